```python
import jax
import jax.numpy as jnp
from jax import lax
import numpy as np

D_MODEL = 1024
BATCH = 8
SEQ = 8192
DEPTH = 4

N_A_LAYERS = DEPTH // 2
N_B_LAYERS = DEPTH - N_A_LAYERS
A_INNER = 2 * D_MODEL
A_HEADS = 4
A_HEAD_DIM = A_INNER // A_HEADS
A_CONV = 4
A_QKV_BLOCK = 4
A_CHUNK = 64
B_HEADS = 16
B_GROUPS = 2
B_HPG = B_HEADS // B_GROUPS
B_DK = 128
B_DV = 128
B_INNER = B_HEADS * B_DV
N_BRANCH = 3
CMP_BLK = 32
CMP_STRIDE = 16
CMP_HIDDEN = 256
SEL_BLK = 64
SEL_TOP = 16
WINDOW = 512
Q_BLOCK = 128
EPS = 1e-6

kernel_name = "yoco_mlstm_nsa_hybrid"


def rmsnorm(x, g):
    xf = x.astype(jnp.float32)
    y = xf * lax.rsqrt(jnp.mean(xf * xf, axis=-1, keepdims=True) + EPS)
    return (y * g.astype(jnp.float32)).astype(x.dtype)


def masked_softmax(s, mask):
    s = jnp.where(mask, s, -jnp.inf)
    m = jnp.max(s, axis=-1, keepdims=True)
    m = jnp.where(jnp.isfinite(m), m, 0.0)
    e = jnp.exp(s - m)
    return e / jnp.maximum(jnp.sum(e, axis=-1, keepdims=True), 1e-30)


def causal_depthwise_conv(x, w, b):
    taps = w.shape[0]
    y = lax.conv_general_dilated(x, w[:, None, :], window_strides=(1,),
                                 padding=[(taps - 1, 0)],
                                 dimension_numbers=("NWC", "WIO", "NWC"),
                                 feature_group_count=x.shape[-1])
    return y + b


def blockdiag(x, w):
    nb, bs, _ = w.shape
    xb = x.reshape(x.shape[:-1] + (nb, bs))
    return jnp.einsum("bsgi,gij->bsgj", xb, w).reshape(x.shape)


def mlstm_cell(q, k, v, i_pre, f_pre):
    bsz, seq, nh, dh = q.shape
    nc = seq // A_CHUNK

    def to_chunks(a):
        a = a.astype(jnp.float32).reshape((bsz, nc, A_CHUNK, nh) + a.shape[3:])
        return jnp.moveaxis(jnp.moveaxis(a, 1, 0), 3, 2)

    qc = to_chunks(q)
    kc = to_chunks(k) * (dh ** -0.5)
    vc = to_chunks(v)
    ic = to_chunks(i_pre)
    lfc = jax.nn.log_sigmoid(to_chunks(f_pre))
    causal = jnp.tril(jnp.ones((A_CHUNK, A_CHUNK), dtype=bool))

    def step(carry, xs):
        c_st, n_st, m_st = carry
        qj, kj, vj, ij, lfj = xs
        b = jnp.cumsum(lfj, axis=-1)
        log_d = jnp.where(causal, b[..., :, None] - b[..., None, :] + ij[..., None, :], -jnp.inf)
        m_inter = b + m_st[..., None]
        m_t = jnp.maximum(jnp.max(log_d, axis=-1), m_inter)
        d = jnp.exp(log_d - m_t[..., None])
        s = jnp.einsum("bhtd,bhsd->bhts", qj, kj) * d
        decay = jnp.exp(m_inter - m_t)
        num = decay[..., None] * jnp.einsum("bhtk,bhkv->bhtv", qj, c_st) + jnp.einsum("bhts,bhsv->bhtv", s, vj)
        den = decay * jnp.einsum("bhtk,bhk->bht", qj, n_st) + jnp.sum(s, axis=-1)
        h = num / jnp.maximum(jnp.abs(den), jnp.exp(-m_t))[..., None]
        b_last = b[..., -1]
        g = b_last[..., None] - b + ij
        m_new = jnp.maximum(b_last + m_st, jnp.max(g, axis=-1))
        w = jnp.exp(g - m_new[..., None])
        carry_decay = jnp.exp(b_last + m_st - m_new)
        c_new = carry_decay[..., None, None] * c_st + jnp.einsum("bhl,bhlk,bhlv->bhkv", w, kj, vj)
        n_new = carry_decay[..., None] * n_st + jnp.einsum("bhl,bhlk->bhk", w, kj)
        return (c_new, n_new, m_new), h

    init = (jnp.zeros((bsz, nh, dh, dh), jnp.float32),
            jnp.zeros((bsz, nh, dh), jnp.float32),
            jnp.zeros((bsz, nh), jnp.float32))
    _, hs = lax.scan(step, init, (qc, kc, vc, ic, lfc))
    hs = jnp.moveaxis(jnp.moveaxis(hs, 0, 1), 2, 3)
    return hs.reshape(bsz, seq, nh, dh)


def mlstm_mixer(h, w_in, conv_w, conv_b, w_q, w_k, w_v, w_gate, b_gate, head_norm, skip, w_out):
    bsz, seq, _ = h.shape
    xm, o_pre, z = jnp.split(h @ w_in, 3, axis=-1)
    xc = jax.nn.silu(causal_depthwise_conv(xm, conv_w, conv_b))
    q = blockdiag(xc, w_q)
    k = blockdiag(xc, w_k)
    v = blockdiag(xm, w_v)
    gates = (q @ w_gate[:A_INNER] + k @ w_gate[A_INNER:2 * A_INNER]
             + v @ w_gate[2 * A_INNER:] + b_gate).astype(jnp.float32)
    i_pre, f_pre = gates[..., :A_HEADS], gates[..., A_HEADS:]

    def heads(a):
        return a.reshape(bsz, seq, A_HEADS, A_HEAD_DIM)

    cell = mlstm_cell(heads(q), heads(k), heads(v), i_pre, f_pre)
    hc = jax.nn.sigmoid(heads(o_pre).astype(jnp.float32)) * cell
    mu = jnp.mean(hc, axis=-1, keepdims=True)
    var = jnp.mean(jnp.square(hc - mu), axis=-1, keepdims=True)
    hn = ((hc - mu) * lax.rsqrt(var + EPS)).reshape(bsz, seq, A_INNER) * head_norm
    y = (hn + skip * xc) * jax.nn.silu(z)
    return y.astype(h.dtype) @ w_out


def compress_blocks(a, pos, w1, b1, w2, b2):
    bsz, seq, g, d = a.shape
    r = CMP_BLK // CMP_STRIDE
    ch = a.reshape(bsz, seq // CMP_STRIDE, CMP_STRIDE, g, d)
    nc = seq // CMP_STRIDE - r + 1
    blk = jnp.concatenate([ch[:, i:i + nc] for i in range(r)], axis=2)
    blk = blk + pos[:, None, :]
    flat = jnp.moveaxis(blk, 3, 2).reshape(bsz, nc, g, CMP_BLK * d)
    return jax.nn.silu(flat @ w1 + b1) @ w2 + b2


def nsa_shared_kv(x, kv_norm, kv_w, cmp_pos, cmp_w1, cmp_b1, cmp_w2, cmp_b2):
    bsz, seq, _ = x.shape
    kv = (rmsnorm(x, kv_norm) @ kv_w).reshape(bsz, seq, 2 * N_BRANCH, B_GROUPS, B_DK)
    kc = compress_blocks(kv[:, :, 0], cmp_pos[0], cmp_w1[0], cmp_b1[0], cmp_w2[0], cmp_b2[0])
    vc = compress_blocks(kv[:, :, 1], cmp_pos[1], cmp_w1[1], cmp_b1[1], cmp_w2[1], cmp_b2[1])
    ns = seq // SEL_BLK

    def sel_blocks(a):
        return a.reshape(bsz, ns, SEL_BLK, B_GROUPS, B_DK).transpose(0, 3, 1, 2, 4)

    ks = sel_blocks(kv[:, :, 2])
    vs = sel_blocks(kv[:, :, 3])
    pad = ((0, 0), (WINDOW, 0), (0, 0), (0, 0))
    kw = jnp.pad(kv[:, :, 4], pad)
    vw = jnp.pad(kv[:, :, 5], pad)
    return (kc, vc, ks, vs, kw, vw)


def cmp_to_sel_overlap(nc, ns):
    i = jnp.arange(nc)[:, None] * CMP_STRIDE
    j = jnp.arange(ns)[None, :] * SEL_BLK
    ov = jnp.minimum(i + CMP_BLK, j + SEL_BLK) - jnp.maximum(i, j)
    return (jnp.maximum(ov, 0) / CMP_STRIDE).astype(jnp.float32)


def alibi_slopes():
    h = jnp.arange(1, B_HEADS + 1, dtype=jnp.float32)
    return (2.0 ** (-8.0 * h / B_HEADS)).reshape(B_GROUPS, B_HPG)


def nsa_mixer(h, kv, w_in, w_out):
    bsz, seq, _ = h.shape
    kc, vc, ks, vs, kw, vw = kv
    nqk = B_HEADS * B_DK
    ngl = N_BRANCH * B_HEADS
    proj = h @ w_in
    q = proj[..., :nqk].reshape(bsz, seq, B_GROUPS, B_HPG, B_DK) * (B_DK ** -0.5)
    gates = jax.nn.sigmoid(proj[..., nqk:nqk + ngl].astype(jnp.float32)).reshape(
        bsz, seq, N_BRANCH, B_GROUPS, B_HPG)
    zg = jax.nn.silu(proj[..., nqk + ngl:]).reshape(bsz, seq, N_BRANCH, B_GROUPS, B_HPG, B_DV)
    nc = kc.shape[1]
    ns = ks.shape[2]
    n_sel = min(SEL_TOP, ns)
    w_ov = cmp_to_sel_overlap(nc, ns)
    slopes = alibi_slopes()
    c_idx = jnp.arange(nc)
    c_end = c_idx * CMP_STRIDE + CMP_BLK - 1
    c_mid = (c_idx * CMP_STRIDE).astype(jnp.float32) + 0.5 * (CMP_BLK - 1)
    j_blk = jnp.arange(ns)
    b_idx = jnp.arange(bsz)[:, None, None, None]
    g_idx = jnp.arange(B_GROUPS)[None, None, :, None]
    nqb = seq // Q_BLOCK

    def to_blocks(a):
        return jnp.moveaxis(a.reshape((bsz, nqb, Q_BLOCK) + a.shape[2:]), 1, 0)

    def query_block(args):
        qb, gb, zb, t0 = args
        t = t0 + jnp.arange(Q_BLOCK)
        tf = t.astype(jnp.float32)
        s = jnp.einsum("bqghd,bngd->bqghn", qb, kc, preferred_element_type=jnp.float32)
        s = s - slopes[:, :, None] * (tf[:, None] - c_mid[None, :])[:, None, None, :]
        p_c = masked_softmax(s, (c_end[None, :] <= t[:, None])[:, None, None, :])
        o_c = jnp.einsum("bqghn,bngd->bqghd", p_c, vc)
        imp = jnp.einsum("bqghn,nj->bqgj", p_c, w_ov)
        cur = (t // SEL_BLK)[:, None]
        forced = (j_blk == 0) | (j_blk == cur) | (j_blk == cur - 1)
        imp = jnp.where(forced[:, None], jnp.inf, jnp.where((j_blk <= cur)[:, None], imp, -jnp.inf))
        _, idx = lax.top_k(imp, n_sel)
        k_sel = ks[b_idx, g_idx, idx]
        v_sel = vs[b_idx, g_idx, idx].reshape(bsz, Q_BLOCK, B_GROUPS, n_sel * SEL_BLK, B_DV)
        pos = idx[..., None] * SEL_BLK + jnp.arange(SEL_BLK)
        dist = t[None, :, None, None, None] - pos
        s = jnp.einsum("bqghd,bqgnld->bqghnl", qb, k_sel, preferred_element_type=jnp.float32)
        s = s - slopes[:, :, None, None] * dist[:, :, :, None].astype(jnp.float32)
        s = s.reshape(bsz, Q_BLOCK, B_GROUPS, B_HPG, n_sel * SEL_BLK)
        mask = (dist >= 0)[:, :, :, None].reshape(bsz, Q_BLOCK, B_GROUPS, 1, n_sel * SEL_BLK)
        o_s = jnp.einsum("bqghk,bqgkd->bqghd", masked_softmax(s, mask), v_sel)
        k_w = lax.dynamic_slice_in_dim(kw, t0, WINDOW + Q_BLOCK, axis=1)
        v_w = lax.dynamic_slice_in_dim(vw, t0, WINDOW + Q_BLOCK, axis=1)
        kpos = t0 - WINDOW + jnp.arange(WINDOW + Q_BLOCK)
        wd = t[:, None] - kpos[None, :]
        wmask = (wd >= 0) & (wd < WINDOW) & (kpos >= 0)[None, :]
        s = jnp.einsum("bqghd,bkgd->bqghk", qb, k_w, preferred_element_type=jnp.float32)
        s = s - slopes[:, :, None] * wd.astype(jnp.float32)[:, None, None, :]
        o_w = jnp.einsum("bqghk,bkgd->bqghd", masked_softmax(s, wmask[:, None, None, :]), v_w)
        o_all = jnp.stack([o_c, o_s, o_w], axis=2)
        o = jnp.sum(gb[..., None] * zb * o_all, axis=2)
        return o.reshape(bsz, Q_BLOCK, B_INNER)

    t0s = jnp.arange(nqb, dtype=jnp.int32) * Q_BLOCK
    out = lax.map(query_block, (to_blocks(q), to_blocks(gates), to_blocks(zg), t0s))
    out = jnp.moveaxis(out, 0, 1).reshape(bsz, seq, B_INNER)
    return out.astype(h.dtype) @ w_out


def setup_inputs(seed: int = 0) -> dict:
    key = jax.random.key(seed)
    keys = iter(jax.random.split(key, 32))

    def nrm(shape, scale):
        return jax.random.normal(next(keys), shape, jnp.float32) * scale

    na, nb = N_A_LAYERS, N_B_LAYERS
    nqb_blocks = A_INNER // A_QKV_BLOCK
    forget_bias = jnp.asarray(np.linspace(3.0, 6.0, A_HEADS), jnp.float32)
    b_in_width = B_HEADS * B_DK + N_BRANCH * B_HEADS + N_BRANCH * B_INNER
    x = nrm((BATCH, SEQ, D_MODEL), 1.0)
    norm_pre = 1.0 + nrm((DEPTH, D_MODEL), 0.02)
    norm_post = 1.0 + nrm((DEPTH, D_MODEL), 0.02)
    a_w_in = nrm((na, D_MODEL, 3 * A_INNER), D_MODEL ** -0.5)
    a_conv_w = nrm((na, A_CONV, A_INNER), A_CONV ** -0.5)
    a_conv_b = nrm((na, A_INNER), 0.01)
    a_w_q = nrm((na, nqb_blocks, A_QKV_BLOCK, A_QKV_BLOCK), A_QKV_BLOCK ** -0.5)
    a_w_k = nrm((na, nqb_blocks, A_QKV_BLOCK, A_QKV_BLOCK), A_QKV_BLOCK ** -0.5)
    a_w_v = nrm((na, nqb_blocks, A_QKV_BLOCK, A_QKV_BLOCK), A_QKV_BLOCK ** -0.5)
    a_w_gate = nrm((na, 3 * A_INNER, 2 * A_HEADS), (3 * A_INNER) ** -0.5)
    a_b_gate = jnp.concatenate([nrm((na, A_HEADS), 0.1),
                                forget_bias[None, :] + nrm((na, A_HEADS), 0.01)], axis=-1)
    a_head_norm = 1.0 + nrm((na, A_INNER), 0.02)
    a_skip = 1.0 + nrm((na, A_INNER), 0.02)
    a_w_out = nrm((na, A_INNER, D_MODEL), A_INNER ** -0.5)
    kv_norm = 1.0 + nrm((D_MODEL,), 0.02)
    kv_w = nrm((D_MODEL, 2 * N_BRANCH * B_GROUPS * B_DK), D_MODEL ** -0.5)
    cmp_pos = nrm((2, CMP_BLK, B_DK), 0.02)
    cmp_w1 = nrm((2, CMP_BLK * B_DK, CMP_HIDDEN), (CMP_BLK * B_DK) ** -0.5)
    cmp_b1 = nrm((2, CMP_HIDDEN), 0.01)
    cmp_w2 = nrm((2, CMP_HIDDEN, B_DK), CMP_HIDDEN ** -0.5)
    cmp_b2 = nrm((2, B_DK), 0.01)
    b_w_in = nrm((nb, D_MODEL, b_in_width), D_MODEL ** -0.5)
    b_w_out = nrm((nb, B_INNER, D_MODEL), B_INNER ** -0.5)
    return {"x": x, "norm_pre": norm_pre, "norm_post": norm_post,
            "a_w_in": a_w_in, "a_conv_w": a_conv_w, "a_conv_b": a_conv_b,
            "a_w_q": a_w_q, "a_w_k": a_w_k, "a_w_v": a_w_v,
            "a_w_gate": a_w_gate, "a_b_gate": a_b_gate, "a_head_norm": a_head_norm,
            "a_skip": a_skip, "a_w_out": a_w_out,
            "kv_norm": kv_norm, "kv_w": kv_w, "cmp_pos": cmp_pos,
            "cmp_w1": cmp_w1, "cmp_b1": cmp_b1, "cmp_w2": cmp_w2, "cmp_b2": cmp_b2,
            "b_w_in": b_w_in, "b_w_out": b_w_out}


def reference(x, norm_pre, norm_post, a_w_in, a_conv_w, a_conv_b, a_w_q, a_w_k, a_w_v,
              a_w_gate, a_b_gate, a_head_norm, a_skip, a_w_out, kv_norm, kv_w, cmp_pos,
              cmp_w1, cmp_b1, cmp_w2, cmp_b2, b_w_in, b_w_out):
    shared_kv = None
    for layer in range(DEPTH):
        h = rmsnorm(x, norm_pre[layer])
        if layer < N_A_LAYERS:
            y = mlstm_mixer(h, a_w_in[layer], a_conv_w[layer], a_conv_b[layer], a_w_q[layer],
                            a_w_k[layer], a_w_v[layer], a_w_gate[layer], a_b_gate[layer],
                            a_head_norm[layer], a_skip[layer], a_w_out[layer])
        else:
            lb = layer - N_A_LAYERS
            y = nsa_mixer(h, shared_kv, b_w_in[lb], b_w_out[lb])
        x = x + rmsnorm(y, norm_post[layer])
        if layer == N_A_LAYERS - 1:
            shared_kv = nsa_shared_kv(x, kv_norm, kv_w, cmp_pos, cmp_w1, cmp_b1, cmp_w2, cmp_b2)
    return x
```

```python
import functools

import jax
import jax.numpy as jnp
from jax import lax
from jax.experimental import pallas as pl
from jax.experimental.pallas import tpu as pltpu

F32 = jnp.float32
BF16 = jnp.bfloat16
EPS = 1e-6

A_HEADS = 4
A_CONV = 4
B_HEADS = 16
B_GROUPS = 2
B_HPG = B_HEADS // B_GROUPS
B_DK = 128
N_BRANCH = 3
CMP_BLK = 32
CMP_STRIDE = 16
CMP_HIDDEN = 256
SEL_BLK = 64
SEL_TOP = 16
WINDOW = 512

LANES = 128
SUBLANES = 8
MXU_DIM = 256
VMEM_LIMIT_BYTES = 56 * 1024 * 1024

A_CHUNK = 256
A_LOCAL_TILE = 256
Q_TILE = 128
SEL_CHUNK = 256
NEG = -1e30


def _params(n_axes):
    return pltpu.CompilerParams(
        dimension_semantics=("arbitrary",) * n_axes,
        vmem_limit_bytes=VMEM_LIMIT_BYTES)


def _sigmoid(v):
    return 1.0 / (1.0 + jnp.exp(-v))


def _silu(v):
    return v * _sigmoid(v)


def _shr(v, pow2):
    assert pow2 & (pow2 - 1) == 0
    return lax.shift_right_logical(v, jnp.int32(pow2.bit_length() - 1))


def _dot(a, b):
    return jnp.dot(a, b, preferred_element_type=F32)


def _dot_nt(a, b):
    return lax.dot_general(a, b, (((1,), (1,)), ((), ())), preferred_element_type=F32)


def _dot_tn(a, b):
    return lax.dot_general(a, b, (((0,), (0,)), ((), ())), preferred_element_type=F32)


def _rms_matmul_body(x_ref, g_ref, w_ref, o_ref, *, scale):
    x = x_ref[...]
    ms = jnp.mean(x * x, axis=-1, keepdims=True)
    h = (x * lax.rsqrt(ms + EPS)) * g_ref[...]
    acc = _dot(h.astype(BF16), w_ref[...])
    if scale != 1.0:
        acc = acc * scale
    o_ref[...] = acc.astype(o_ref.dtype)


def _pick_tile(n, candidates):
    for c in candidates:
        if n % c == 0:
            return c
    return n


def _rms_matmul(x, g, w, out_dtype, scale=1.0):
    t, d = x.shape
    n = w.shape[1]
    tm = _pick_tile(t, (1024, 512, 256, 128))
    tn = _pick_tile(n, (2048, 1536, 1280, 1024, 896, 768, 512, 256, 128))
    return pl.pallas_call(
        functools.partial(_rms_matmul_body, scale=scale),
        grid=(n // tn, t // tm),
        in_specs=[pl.BlockSpec((tm, d), lambda j, i: (i, 0)),
                  pl.BlockSpec((1, d), lambda j, i: (0, 0)),
                  pl.BlockSpec((d, tn), lambda j, i: (0, j))],
        out_specs=pl.BlockSpec((tm, tn), lambda j, i: (i, j)),
        out_shape=jax.ShapeDtypeStruct((t, n), out_dtype),
        compiler_params=_params(2),
        name="rms_matmul",
    )(x, g.reshape(1, d).astype(F32), w)


def _out_proj_body(y_ref, w_ref, g_ref, x_ref, o_ref):
    r = _dot(y_ref[...], w_ref[...])
    ms = jnp.mean(r * r, axis=-1, keepdims=True)
    o_ref[...] = x_ref[...] + (r * lax.rsqrt(ms + EPS)) * g_ref[...]


def _out_proj(y, w, g, x):
    t, k = y.shape
    d = w.shape[1]
    tm = _pick_tile(t, (512, 256, 128))
    return pl.pallas_call(
        _out_proj_body,
        grid=(t // tm,),
        in_specs=[pl.BlockSpec((tm, k), lambda i: (i, 0)),
                  pl.BlockSpec((k, d), lambda i: (0, 0)),
                  pl.BlockSpec((1, d), lambda i: (0, 0)),
                  pl.BlockSpec((tm, d), lambda i: (i, 0))],
        out_specs=pl.BlockSpec((tm, d), lambda i: (i, 0)),
        out_shape=jax.ShapeDtypeStruct((t, d), F32),
        compiler_params=_params(1),
        name="out_proj",
    )(y, w, g.reshape(1, d).astype(F32), x)


HALO = SUBLANES


def _a_local_body(xm_ref, cw_ref, cb_ref, bdq_ref, bdk_ref, bdv_ref,
                  wgq_ref, wgk_ref, wgv_ref, bg_ref,
                  q_ref, k_ref, v_ref, xc_ref, gates_ref, ext_ref,
                  *, tl, tiles_per_seq, kscale):
    i = pl.program_id(0)
    c = xm_ref.shape[1]
    first = lax.rem(i, tiles_per_seq) == 0

    @pl.when(first)
    def _():
        ext_ref[0:HALO, :] = jnp.zeros((HALO, c), F32)

    @pl.when(jnp.logical_not(first))
    def _():
        ext_ref[0:HALO, :] = ext_ref[tl:tl + HALO, :]

    xm = xm_ref[...]
    ext_ref[HALO:HALO + tl, :] = xm
    acc = jnp.broadcast_to(cb_ref[...], (tl, c))
    for j in range(A_CONV):
        off = HALO - (A_CONV - 1) + j
        acc = acc + cw_ref[j:j + 1, :] * ext_ref[off:off + tl, :]
    xc = _silu(acc)
    xcb = xc.astype(BF16)
    xmb = xm.astype(BF16)
    xc_ref[...] = xcb
    gates = jnp.broadcast_to(bg_ref[...], (tl, LANES))
    for s in range(c // MXU_DIM):
        sl = slice(MXU_DIM * s, MXU_DIM * (s + 1))
        qb = _dot(xcb[:, sl], bdq_ref[s]).astype(BF16)
        kf = _dot(xcb[:, sl], bdk_ref[s])
        kb = kf.astype(BF16)
        vb = _dot(xmb[:, sl], bdv_ref[s]).astype(BF16)
        gates = gates + _dot(qb, wgq_ref[sl, :]) + _dot(kb, wgk_ref[sl, :]) + _dot(vb, wgv_ref[sl, :])
        q_ref[:, sl] = qb
        k_ref[:, sl] = (kf * kscale).astype(BF16)
        v_ref[:, sl] = vb
    gates_ref[...] = gates


def _a_local(xm_src, seq, cw, cb, bdq, bdk, bdv, wgq, wgk, wgv, bg, kscale):
    t = xm_src.shape[0]
    c = cw.shape[1]
    tl = A_LOCAL_TILE
    nslab = c // MXU_DIM
    full = lambda shape: pl.BlockSpec(shape, lambda i: (0,) * len(shape))
    tok = lambda w: pl.BlockSpec((tl, w), lambda i: (i, 0))
    return pl.pallas_call(
        functools.partial(_a_local_body, tl=tl, tiles_per_seq=seq // tl, kscale=kscale),
        grid=(t // tl,),
        in_specs=[tok(c), full((A_CONV, c)), full((1, c)),
                  full((nslab, MXU_DIM, MXU_DIM)), full((nslab, MXU_DIM, MXU_DIM)),
                  full((nslab, MXU_DIM, MXU_DIM)),
                  full((c, LANES)), full((c, LANES)), full((c, LANES)), full((1, LANES))],
        out_specs=[tok(c), tok(c), tok(c), tok(c), tok(LANES)],
        out_shape=[jax.ShapeDtypeStruct((t, c), BF16)] * 4 + [jax.ShapeDtypeStruct((t, LANES), F32)],
        scratch_shapes=[pltpu.VMEM((tl + 2 * HALO, c), F32)],
        compiler_params=_params(1),
        name="a_local",
    )(xm_src, cw, cb, bdq, bdk, bdv, wgq, wgk, wgv, bg)


def _a_cell_body(q_ref, k_ref, v_ref, xc_ref, o_ref, z_ref, gates_ref, hn_ref, skip_ref,
                 y_ref, c_ref, n_ref, m_ref, *, chunk, dh):
    cidx = pl.program_id(1)
    nh = A_HEADS

    @pl.when(cidx == 0)
    def _():
        c_ref[...] = jnp.zeros(c_ref.shape, F32)
        n_ref[...] = jnp.zeros(n_ref.shape, F32)
        m_ref[...] = jnp.zeros(m_ref.shape, F32)

    gates = gates_ref[...]
    lf = jnp.minimum(gates, 0.0) - jnp.log(1.0 + jnp.exp(-jnp.abs(gates)))
    r_i = lax.broadcasted_iota(jnp.int32, (chunk, chunk), 0)
    c_i = lax.broadcasted_iota(jnp.int32, (chunk, chunk), 1)
    causal = r_i >= c_i
    tril = causal.astype(BF16)
    hi = lf.astype(BF16)
    r1 = lf - hi.astype(F32)
    mid = r1.astype(BF16)
    lo = (r1 - mid.astype(F32)).astype(BF16)
    b_col = _dot(tril, hi) + _dot(tril, mid) + _dot(tril, lo)
    g_t = gates.T
    b_t = b_col.T

    for h in range(nh):
        hs = slice(h * dh, (h + 1) * dh)
        qh = q_ref[:, hs]
        kh = k_ref[:, hs]
        vh = v_ref[:, hs]
        i_row = g_t[h:h + 1, :]
        b_row = b_t[nh + h:nh + h + 1, :]
        i_c = gates[:, h:h + 1]
        b_c = b_col[:, nh + h:nh + h + 1]
        m_b = m_ref[h:h + 1, 0:1]
        log_d = jnp.where(causal, b_c - b_row + i_row, -jnp.inf)
        m_inter = b_c + m_b
        m_t = jnp.maximum(jnp.max(log_d, axis=1, keepdims=True), m_inter)
        d = jnp.exp(log_d - m_t)
        s = _dot_nt(qh, kh) * d
        decay = jnp.exp(m_inter - m_t)
        inter = _dot(qh, c_ref[h].astype(BF16))
        intra = _dot(s.astype(BF16), vh)
        num = decay * inter + intra
        qn = jnp.sum(qh.astype(F32) * n_ref[h:h + 1, :], axis=1, keepdims=True)
        den = decay * qn + jnp.sum(s, axis=1, keepdims=True)
        hcell = num * (1.0 / jnp.maximum(jnp.abs(den), jnp.exp(-m_t)))
        b_last = b_col[chunk - 1:chunk, nh + h:nh + h + 1]
        m_new = jnp.maximum(b_last + m_b,
                            jnp.max(b_last - b_row + i_row, axis=1, keepdims=True))
        w_c = jnp.exp(b_last - b_c + i_c - m_new)
        carry = jnp.exp(b_last + m_b - m_new)
        wv = (vh.astype(F32) * w_c).astype(BF16)
        c_ref[h] = carry * c_ref[h] + _dot_tn(kh, wv)
        n_ref[h:h + 1, :] = carry * n_ref[h:h + 1, :] + jnp.sum(
            kh.astype(F32) * w_c, axis=0, keepdims=True)
        m_ref[h:h + 1, :] = jnp.broadcast_to(m_new, (1, LANES))
        hc = _sigmoid(o_ref[:, hs]) * hcell
        mu = jnp.mean(hc, axis=1, keepdims=True)
        cen = hc - mu
        var = jnp.mean(cen * cen, axis=1, keepdims=True)
        hn = cen * lax.rsqrt(var + EPS) * hn_ref[:, hs]
        y = (hn + skip_ref[:, hs] * xc_ref[:, hs].astype(F32)) * _silu(z_ref[:, hs])
        y_ref[:, hs] = y.astype(BF16)


def _a_cell(q, k, v, xc, proj, gates, head_norm, skip, batch, seq):
    t, c = q.shape
    chunk = A_CHUNK
    nc = seq // chunk
    dh = c // A_HEADS
    tokc = lambda j: pl.BlockSpec((chunk, c), lambda b, i, j=j: (b * nc + i, j))
    full = lambda shape: pl.BlockSpec(shape, lambda b, i: (0,) * len(shape))
    return pl.pallas_call(
        functools.partial(_a_cell_body, chunk=chunk, dh=dh),
        grid=(batch, nc),
        in_specs=[tokc(0), tokc(0), tokc(0), tokc(0), tokc(1), tokc(2),
                  pl.BlockSpec((chunk, LANES), lambda b, i: (b * nc + i, 0)),
                  full((1, c)), full((1, c))],
        out_specs=tokc(0),
        out_shape=jax.ShapeDtypeStruct((t, c), BF16),
        scratch_shapes=[pltpu.VMEM((A_HEADS, dh, dh), F32),
                        pltpu.VMEM((SUBLANES, dh), F32),
                        pltpu.VMEM((SUBLANES, LANES), F32)],
        compiler_params=_params(2),
        name="a_cell",
    )(q, k, v, xc, proj, proj, gates, head_norm, skip)


def _kv_compress_body(xa_ref, xb_ref, pa_ref, pb_ref, w1a_ref, w1b_ref, b1_ref, w2_ref, b2_ref,
                      o_ref):
    xa = (xa_ref[0] + pa_ref[0]).astype(BF16)
    xb = (xb_ref[0] + pb_ref[0]).astype(BF16)
    h1 = _dot(xa, w1a_ref[0]) + _dot(xb, w1b_ref[0]) + b1_ref[0]
    h1 = _silu(h1)
    o_ref[0] = (_dot(h1.astype(BF16), w2_ref[0]) + b2_ref[0]).astype(o_ref.dtype)


def _kv_compress(xa, xb, pos_a, pos_b, w1a, w1b, b1, w2, b2, per_kind):
    n, nr, half = xa.shape
    tr = _pick_tile(nr, (256, 128))
    hid = w1a.shape[2]
    dk = w2.shape[2]
    row = lambda: pl.BlockSpec((1, tr, half), lambda p, r: (p, r, 0))
    kind = lambda shape: pl.BlockSpec((1,) + shape, lambda p, r: (p // per_kind, 0, 0))
    return pl.pallas_call(
        _kv_compress_body,
        grid=(n, nr // tr),
        in_specs=[row(), row(), kind((1, half)), kind((1, half)),
                  kind((half, hid)), kind((half, hid)), kind((1, hid)),
                  kind((hid, dk)), kind((1, dk))],
        out_specs=pl.BlockSpec((1, tr, dk), lambda p, r: (p, r, 0)),
        out_shape=jax.ShapeDtypeStruct((n, nr, dk), BF16),
        compiler_params=_params(2),
        name="kv_compress",
    )(xa, xb, pos_a, pos_b, w1a, w1b, b1, w2, b2)


def _stack_heads(q):
    return jnp.concatenate([q[:, h * B_DK:(h + 1) * B_DK] for h in range(B_HPG)], axis=0)


def _row_meta(tq, t0, g):
    rows = B_HPG * tq
    r = lax.broadcasted_iota(jnp.int32, (rows, 1), 0)
    head = _shr(r, tq)
    t_rows = t0 + (r - head * tq)
    slope = jnp.exp2(-0.5 * (g * B_HPG + head + 1).astype(F32))
    return t_rows, slope


def _nsa_cmp_body(q_ref, kc_ref, vc_ref, zg_ref, gl_ref, wov_ref,
                  oc_ref, sel_ref, any_ref, *, tq, n_sel):
    g = pl.program_id(1)
    t0 = pl.program_id(2) * tq
    qs = _stack_heads(q_ref[...])
    kc = kc_ref[0]
    ncp = kc.shape[0]
    nsp = wov_ref.shape[0]
    t_rows, slope = _row_meta(tq, t0, g)
    n = lax.broadcasted_iota(jnp.int32, (1, ncp), 1)
    c_mid = (n * CMP_STRIDE).astype(F32) + 0.5 * (CMP_BLK - 1)
    c_end = n * CMP_STRIDE + (CMP_BLK - 1)
    s = _dot_nt(qs, kc) - slope * (t_rows.astype(F32) - c_mid)
    s = jnp.where(c_end <= t_rows, s, -jnp.inf)
    m = jnp.max(s, axis=1, keepdims=True)
    m = jnp.where(m == -jnp.inf, 0.0, m)
    e = jnp.exp(s - m)
    p = e * (1.0 / jnp.maximum(jnp.sum(e, axis=1, keepdims=True), 1e-30))
    o = _dot(p.astype(BF16), vc_ref[0])
    psum = p[0:tq]
    for h in range(1, B_HPG):
        psum = psum + p[h * tq:(h + 1) * tq]
    p_hi = psum.astype(BF16)
    p_lo = (psum - p_hi.astype(F32)).astype(BF16)
    wov = wov_ref[...]
    imp = _dot_nt(wov, p_hi) + _dot_nt(wov, p_lo)
    j = lax.broadcasted_iota(jnp.int32, (nsp, tq), 0)
    cur = _shr(t0 + lax.broadcasted_iota(jnp.int32, (1, tq), 1), SEL_BLK)
    forced = (j == 0) | (j == cur) | (j == cur - 1)
    val = jnp.where(forced, jnp.inf, jnp.where(j <= cur, imp, -jnp.inf))
    sel_t = jnp.zeros((nsp, tq), F32)
    for _ in range(n_sel):
        mx = jnp.max(val, axis=0, keepdims=True)
        cand = jnp.where((val == mx) & (mx > -jnp.inf), j, nsp)
        pick = j == jnp.min(cand, axis=0, keepdims=True)
        sel_t = jnp.where(pick, 1.0, sel_t)
        val = jnp.where(pick, -jnp.inf, val)
    sel = sel_t.T
    sel_ref[0, 0] = sel.astype(sel_ref.dtype)
    any_ref[0, 0, 0] = jnp.broadcast_to(jnp.max(sel, axis=0, keepdims=True), (SUBLANES, nsp))
    gate = _sigmoid(gl_ref[...])
    for h in range(B_HPG):
        hs = slice(h * B_DK, (h + 1) * B_DK)
        oc_ref[:, hs] = gate[:, h:h + 1] * _silu(zg_ref[:, hs]) * o[h * tq:(h + 1) * tq]


def _nsa_cmp(q, kvc, proj_r, wov_t, batch, seq, n_sel):
    t = q.shape[0]
    tq = Q_TILE
    nqb = seq // tq
    bg = batch * B_GROUPS
    ncp = kvc.shape[1]
    nsp = wov_t.shape[0]
    gw = B_HPG * B_DK
    zoff = 0
    goff = (N_BRANCH * B_HEADS * B_DK) // LANES
    return pl.pallas_call(
        functools.partial(_nsa_cmp_body, tq=tq, n_sel=n_sel),
        grid=(batch, B_GROUPS, nqb),
        in_specs=[pl.BlockSpec((tq, gw), lambda b, g, i: (b * nqb + i, g)),
                  pl.BlockSpec((1, ncp, B_DK), lambda b, g, i: (b * B_GROUPS + g, 0, 0)),
                  pl.BlockSpec((1, ncp, B_DK), lambda b, g, i: (bg + b * B_GROUPS + g, 0, 0)),
                  pl.BlockSpec((tq, gw), lambda b, g, i: (b * nqb + i, zoff + g)),
                  pl.BlockSpec((tq, LANES), lambda b, g, i: (b * nqb + i, goff + g)),
                  pl.BlockSpec((nsp, ncp), lambda b, g, i: (0, 0))],
        out_specs=[pl.BlockSpec((tq, gw), lambda b, g, i: (b * nqb + i, g)),
                   pl.BlockSpec((1, 1, tq, nsp), lambda b, g, i: (b, g, i, 0)),
                   pl.BlockSpec((1, 1, 1, SUBLANES, nsp), lambda b, g, i: (b, g, i, 0, 0))],
        out_shape=[jax.ShapeDtypeStruct((t, B_GROUPS * gw), F32),
                   jax.ShapeDtypeStruct((batch, B_GROUPS, seq, nsp), BF16),
                   jax.ShapeDtypeStruct((batch, B_GROUPS, nqb, SUBLANES, nsp), F32)],
        compiler_params=_params(3),
        name="nsa_cmp",
    )(q, kvc, kvc, proj_r, proj_r, wov_t)


def _online_update(qs, k_c, v_c, bias, keep_masks, m_ref, l_ref, acc_ref):
    s = _dot_nt(qs, k_c) - bias
    for km in keep_masks:
        s = jnp.where(km, s, NEG)
    m_prev = m_ref[:, 0:1]
    m_new = jnp.maximum(m_prev, jnp.max(s, axis=1, keepdims=True))
    p = jnp.where(s > 0.5 * NEG, jnp.exp(s - m_new), 0.0)
    alpha = jnp.exp(m_prev - m_new)
    l_new = alpha * l_ref[:, 0:1] + jnp.sum(p, axis=1, keepdims=True)
    acc_ref[...] = alpha * acc_ref[...] + _dot(p.astype(BF16), v_c)
    l_ref[...] = jnp.broadcast_to(l_new, l_ref.shape)
    m_ref[...] = jnp.broadcast_to(m_new, m_ref.shape)


def _softmax_reset(m_ref, l_ref, acc_ref):
    m_ref[...] = jnp.full(m_ref.shape, NEG, F32)
    l_ref[...] = jnp.zeros(l_ref.shape, F32)
    acc_ref[...] = jnp.zeros(acc_ref.shape, F32)


def _nsa_sw_body(flags_ref, q_ref, sel_ref, ks_ref, vs_ref, kw_ref, vw_ref,
                 zs_ref, zw_ref, gl_ref, oc_ref, out_ref, m_ref, l_ref, acc_ref,
                 *, tq, nqb, nch):
    b = pl.program_id(0)
    g = pl.program_id(1)
    i = pl.program_id(2)
    t0 = i * tq
    qs = _stack_heads(q_ref[...])
    t_rows, slope = _row_meta(tq, t0, g)
    sel = sel_ref[0, 0]
    nsp = sel.shape[1]
    ch = SEL_CHUNK
    bpc = ch // SEL_BLK
    fbase = ((b * B_GROUPS + g) * nqb + i) * nch

    _softmax_reset(m_ref, l_ref, acc_ref)

    def sel_chunk(c, carry):
        @pl.when(flags_ref[fbase + c] > 0)
        def _():
            start = pl.multiple_of(c * ch, ch)
            jj = lax.broadcasted_iota(jnp.int32, (nsp, ch), 0)
            xx = lax.broadcasted_iota(jnp.int32, (nsp, ch), 1)
            expand = (jj == c * bpc + _shr(xx, SEL_BLK)).astype(BF16)
            chosen = _dot(sel, expand)
            chosen = jnp.concatenate([chosen] * B_HPG, axis=0)
            kpos = start + lax.broadcasted_iota(jnp.int32, (1, ch), 1)
            dist = t_rows - kpos
            _online_update(qs, ks_ref[0, 0, pl.ds(start, ch), :], vs_ref[0, 0, pl.ds(start, ch), :],
                           slope * dist.astype(F32), (chosen > 0.5, dist >= 0),
                           m_ref, l_ref, acc_ref)
        return carry

    lax.fori_loop(0, _shr(t0 + tq + ch - 1, ch), sel_chunk, 0)
    o_s = acc_ref[...] * (1.0 / jnp.maximum(l_ref[:, 0:1], 1e-30))

    _softmax_reset(m_ref, l_ref, acc_ref)
    width = tq
    for off in range(-WINDOW, tq, width):
        start = t0 + off

        @pl.when(start >= 0)
        def _(start=start):
            st = pl.multiple_of(start, Q_TILE)
            kpos = st + lax.broadcasted_iota(jnp.int32, (1, width), 1)
            dist = t_rows - kpos
            _online_update(qs, kw_ref[0, 0, pl.ds(st, width), :], vw_ref[0, 0, pl.ds(st, width), :],
                           slope * dist.astype(F32), (dist >= 0, dist < WINDOW),
                           m_ref, l_ref, acc_ref)

    o_w = acc_ref[...] * (1.0 / jnp.maximum(l_ref[:, 0:1], 1e-30))

    gate = _sigmoid(gl_ref[...])
    for h in range(B_HPG):
        hs = slice(h * B_DK, (h + 1) * B_DK)
        rs = slice(h * tq, (h + 1) * tq)
        merged = (oc_ref[:, hs]
                  + gate[:, B_HPG + h:B_HPG + h + 1] * _silu(zs_ref[:, hs]) * o_s[rs]
                  + gate[:, 2 * B_HPG + h:2 * B_HPG + h + 1] * _silu(zw_ref[:, hs]) * o_w[rs])
        out_ref[:, hs] = merged.astype(out_ref.dtype)


def _nsa_sw(flags, q, sel, ks, vs, kw, vw, proj_r, oc, batch, seq):
    t = q.shape[0]
    tq = Q_TILE
    nqb = seq // tq
    nch = seq // SEL_CHUNK
    nsp = sel.shape[3]
    gw = B_HPG * B_DK
    goff = (N_BRANCH * B_HEADS * B_DK) // LANES
    rows = B_HPG * tq
    tok = lambda j: pl.BlockSpec((tq, gw), lambda b, g, i, f, j=j: (b * nqb + i, j * B_GROUPS + g))
    res = lambda: pl.BlockSpec((1, 1, seq, B_DK), lambda b, g, i, f: (b, g, 0, 0))
    grid_spec = pltpu.PrefetchScalarGridSpec(
        num_scalar_prefetch=1,
        grid=(batch, B_GROUPS, nqb),
        in_specs=[tok(0),
                  pl.BlockSpec((1, 1, tq, nsp), lambda b, g, i, f: (b, g, i, 0)),
                  res(), res(), res(), res(),
                  tok(1), tok(2),
                  pl.BlockSpec((tq, LANES), lambda b, g, i, f: (b * nqb + i, goff + g)),
                  tok(0)],
        out_specs=tok(0),
        scratch_shapes=[pltpu.VMEM((rows, LANES), F32),
                        pltpu.VMEM((rows, LANES), F32),
                        pltpu.VMEM((rows, B_DK), F32)])
    return pl.pallas_call(
        functools.partial(_nsa_sw_body, tq=tq, nqb=nqb, nch=nch),
        grid_spec=grid_spec,
        out_shape=jax.ShapeDtypeStruct((t, B_GROUPS * gw), BF16),
        compiler_params=_params(3),
        name="nsa_sw",
    )(flags, q, sel, ks, vs, kw, vw, proj_r, proj_r, proj_r, oc)


def _blockdiag_dense(w):
    nb, bs, _ = w.shape
    per = MXU_DIM // bs
    w4 = w.reshape(nb // per, per, bs, bs)
    dense = jnp.einsum("sgij,gh->sgihj", w4, jnp.eye(per, dtype=w.dtype))
    return dense.reshape(nb // per, MXU_DIM, MXU_DIM).astype(BF16)


def _pad_cols(w, width):
    return jnp.pad(w, ((0, 0), (0, width - w.shape[1])))


def _overlap_matrix_t(nr, nsp, nc, ns):
    i = jnp.arange(nr)[None, :] * CMP_STRIDE
    j = jnp.arange(nsp)[:, None] * SEL_BLK
    ov = jnp.minimum(i + CMP_BLK, j + SEL_BLK) - jnp.maximum(i, j)
    ov = jnp.maximum(ov, 0) / CMP_STRIDE
    live = (jnp.arange(nr)[None, :] < nc) & (jnp.arange(nsp)[:, None] < ns)
    return jnp.where(live, ov, 0).astype(BF16)


def _mlstm_layer(xf, batch, seq, g_pre, g_post, w_in, conv_w, conv_b, w_q, w_k, w_v,
                 w_gate, b_gate, head_norm, skip, w_out):
    c = conv_w.shape[1]
    dh = c // A_HEADS
    proj = _rms_matmul(xf, g_pre, w_in.astype(BF16), F32)
    wg = [_pad_cols(w_gate[j * c:(j + 1) * c], LANES).astype(BF16) for j in range(3)]
    bg = _pad_cols(b_gate.reshape(1, -1), LANES).astype(F32)
    q, k, v, xc, gates = _a_local(
        proj, seq, conv_w.astype(F32), conv_b.reshape(1, c).astype(F32),
        _blockdiag_dense(w_q), _blockdiag_dense(w_k), _blockdiag_dense(w_v),
        wg[0], wg[1], wg[2], bg, float(dh) ** -0.5)
    y = _a_cell(q, k, v, xc, proj, gates, head_norm.reshape(1, c).astype(F32),
                skip.reshape(1, c).astype(F32), batch, seq)
    return _out_proj(y, w_out.astype(BF16), g_post, xf)


def _shared_kv(xf, batch, seq, kv_norm, kv_w, cmp_pos, cmp_w1, cmp_b1, cmp_w2, cmp_b2):
    kv = _rms_matmul(xf, kv_norm, kv_w.astype(BF16), F32)
    kv = kv.reshape(batch, seq, 2 * N_BRANCH, B_GROUPS, B_DK)
    nr = seq // CMP_STRIDE
    half = CMP_STRIDE * B_DK
    xc = jnp.transpose(kv[:, :, 0:2], (2, 0, 3, 1, 4))
    xa = xc.reshape(2 * batch * B_GROUPS, nr, half)
    xb = jnp.concatenate([xa[:, 1:], jnp.zeros_like(xa[:, :1])], axis=1)
    pos_a = cmp_pos[:, :CMP_STRIDE].reshape(2, 1, half).astype(F32)
    pos_b = cmp_pos[:, CMP_STRIDE:].reshape(2, 1, half).astype(F32)
    kvc = _kv_compress(xa, xb, pos_a, pos_b,
                       cmp_w1[:, :half].astype(BF16), cmp_w1[:, half:].astype(BF16),
                       cmp_b1.reshape(2, 1, -1).astype(F32), cmp_w2.astype(BF16),
                       cmp_b2.reshape(2, 1, -1).astype(F32), batch * B_GROUPS)
    rest = jnp.transpose(kv[:, :, 2:], (2, 0, 3, 1, 4)).astype(BF16)
    return kvc, rest[0], rest[1], rest[2], rest[3]


def _nsa_layer(xf, batch, seq, g_pre, g_post, w_in, w_out, shared, wov_t, n_sel):
    kvc, ks, vs, kw, vw = shared
    nqk = B_HEADS * B_DK
    ngl = N_BRANCH * B_HEADS
    w_q = w_in[:, :nqk].astype(BF16)
    w_gl = w_in[:, nqk:nqk + ngl].reshape(-1, N_BRANCH, B_GROUPS, B_HPG)
    w_gl = jnp.transpose(w_gl, (0, 2, 1, 3)).reshape(-1, B_GROUPS, N_BRANCH * B_HPG)
    w_gl = jnp.pad(w_gl, ((0, 0), (0, 0), (0, LANES - N_BRANCH * B_HPG))).reshape(-1, B_GROUPS * LANES)
    w_r = jnp.concatenate([w_in[:, nqk + ngl:], w_gl], axis=1).astype(BF16)
    q = _rms_matmul(xf, g_pre, w_q, BF16, scale=float(B_DK) ** -0.5)
    proj_r = _rms_matmul(xf, g_pre, w_r, F32)
    oc, sel, blk_any = _nsa_cmp(q, kvc, proj_r, wov_t, batch, seq, n_sel)
    ns = seq // SEL_BLK
    bpc = SEL_CHUNK // SEL_BLK
    flags = blk_any[:, :, :, 0, :ns].reshape(batch, B_GROUPS, seq // Q_TILE, ns // bpc, bpc)
    flags = (jnp.max(flags, axis=-1) > 0).astype(jnp.int32).reshape(-1)
    out = _nsa_sw(flags, q, sel, ks, vs, kw, vw, proj_r, oc, batch, seq)
    return _out_proj(out, w_out.astype(BF16), g_post, xf)


def kernel(x, norm_pre, norm_post, a_w_in, a_conv_w, a_conv_b, a_w_q, a_w_k, a_w_v, a_w_gate,
           a_b_gate, a_head_norm, a_skip, a_w_out, kv_norm, kv_w, cmp_pos, cmp_w1, cmp_b1,
           cmp_w2, cmp_b2, b_w_in, b_w_out):
    batch, seq, d = x.shape
    na = a_w_in.shape[0]
    nb = b_w_in.shape[0]
    assert seq % (CMP_STRIDE * LANES) == 0 and seq % SEL_CHUNK == 0 and seq % A_CHUNK == 0
    xf = x.reshape(batch * seq, d).astype(F32)
    for l in range(na):
        xf = _mlstm_layer(xf, batch, seq, norm_pre[l], norm_post[l], a_w_in[l], a_conv_w[l],
                          a_conv_b[l], a_w_q[l], a_w_k[l], a_w_v[l], a_w_gate[l], a_b_gate[l],
                          a_head_norm[l], a_skip[l], a_w_out[l])
    shared = _shared_kv(xf, batch, seq, kv_norm, kv_w, cmp_pos, cmp_w1, cmp_b1, cmp_w2, cmp_b2)
    nr = seq // CMP_STRIDE
    ns = seq // SEL_BLK
    nsp = -(-ns // LANES) * LANES
    wov_t = _overlap_matrix_t(nr, nsp, nr - CMP_BLK // CMP_STRIDE + 1, ns)
    for l in range(nb):
        xf = _nsa_layer(xf, batch, seq, norm_pre[na + l], norm_post[na + l], b_w_in[l],
                        b_w_out[l], shared, wov_t, min(SEL_TOP, ns))
    return xf.reshape(batch, seq, d).astype(x.dtype)
```

```python
import functools

import jax
import jax.numpy as jnp
from jax import lax
from jax.experimental import pallas as pl
from jax.experimental.pallas import tpu as pltpu

F32 = jnp.float32
BF16 = jnp.bfloat16
EPS = 1e-6

A_HEADS = 4
A_CONV = 4
B_HEADS = 16
B_GROUPS = 2
B_HPG = B_HEADS // B_GROUPS
B_DK = 128
N_BRANCH = 3
CMP_BLK = 32
CMP_STRIDE = 16
CMP_HIDDEN = 256
SEL_BLK = 64
SEL_TOP = 16
WINDOW = 512

LANES = 128
SUBLANES = 8
MXU_DIM = 256
VMEM_LIMIT_BYTES = 56 * 1024 * 1024

A_CHUNK = 256
A_LOCAL_TILE = 256
Q_TILE = 128
SEL_CHUNK = 256
NEG = -1e30
M_INIT = -1e20
assert B_DK == LANES


def _params(n_axes):
    return pltpu.CompilerParams(
        dimension_semantics=("arbitrary",) * n_axes,
        vmem_limit_bytes=VMEM_LIMIT_BYTES)


def _sigmoid(v):
    return 1.0 / (1.0 + jnp.exp(-v))


def _silu(v):
    return v * _sigmoid(v)


def _shr(v, pow2):
    assert pow2 & (pow2 - 1) == 0
    return lax.shift_right_logical(v, jnp.int32(pow2.bit_length() - 1))


def _dot(a, b):
    return jnp.dot(a, b, preferred_element_type=F32)


def _dot_nt(a, b):
    return lax.dot_general(a, b, (((1,), (1,)), ((), ())), preferred_element_type=F32)


def _dot_tn(a, b):
    return lax.dot_general(a, b, (((0,), (0,)), ((), ())), preferred_element_type=F32)


def _rms_matmul_body(x_ref, g_ref, w_ref, o_ref, *, scale):
    x = x_ref[...]
    ms = jnp.mean(x * x, axis=-1, keepdims=True)
    h = (x * lax.rsqrt(ms + EPS)) * g_ref[...]
    acc = _dot(h.astype(BF16), w_ref[...])
    if scale != 1.0:
        acc = acc * scale
    if len(o_ref.shape) == 2:
        o_ref[...] = acc.astype(o_ref.dtype)
    else:
        for j in range(o_ref.shape[0]):
            o_ref[j] = acc[:, j * LANES:(j + 1) * LANES].astype(o_ref.dtype)


def _pick_tile(n, candidates):
    for c in candidates:
        if n % c == 0:
            return c
    return n


def _rms_matmul(x, g, w, out_dtype, scale=1.0, split=False):
    t, d = x.shape
    n = w.shape[1]
    tm = _pick_tile(t, (1024, 512, 256, 128))
    tn = _pick_tile(n, (2048, 1536, 1280, 1024, 896, 768, 512, 256, 128))
    if split:
        out_spec = pl.BlockSpec((tn // LANES, tm, LANES), lambda j, i: (j, i, 0))
        out_shape = jax.ShapeDtypeStruct((n // LANES, t, LANES), out_dtype)
    else:
        out_spec = pl.BlockSpec((tm, tn), lambda j, i: (i, j))
        out_shape = jax.ShapeDtypeStruct((t, n), out_dtype)
    return pl.pallas_call(
        functools.partial(_rms_matmul_body, scale=scale),
        grid=(n // tn, t // tm),
        in_specs=[pl.BlockSpec((tm, d), lambda j, i: (i, 0)),
                  pl.BlockSpec((1, d), lambda j, i: (0, 0)),
                  pl.BlockSpec((d, tn), lambda j, i: (0, j))],
        out_specs=out_spec,
        out_shape=out_shape,
        compiler_params=_params(2),
        name="rms_matmul",
    )(x, g.reshape(1, d).astype(F32), w)


def _out_proj_body(y_ref, w_ref, g_ref, x_ref, o_ref):
    r = _dot(y_ref[...], w_ref[...])
    ms = jnp.mean(r * r, axis=-1, keepdims=True)
    o_ref[...] = x_ref[...] + (r * lax.rsqrt(ms + EPS)) * g_ref[...]


def _out_proj(y, w, g, x):
    t, k = y.shape
    d = w.shape[1]
    tm = _pick_tile(t, (512, 256, 128))
    return pl.pallas_call(
        _out_proj_body,
        grid=(t // tm,),
        in_specs=[pl.BlockSpec((tm, k), lambda i: (i, 0)),
                  pl.BlockSpec((k, d), lambda i: (0, 0)),
                  pl.BlockSpec((1, d), lambda i: (0, 0)),
                  pl.BlockSpec((tm, d), lambda i: (i, 0))],
        out_specs=pl.BlockSpec((tm, d), lambda i: (i, 0)),
        out_shape=jax.ShapeDtypeStruct((t, d), F32),
        compiler_params=_params(1),
        name="out_proj",
    )(y, w, g.reshape(1, d).astype(F32), x)


HALO = SUBLANES


def _a_local_body(xm_ref, cw_ref, cb_ref, bdq_ref, bdk_ref, bdv_ref,
                  wgq_ref, wgk_ref, wgv_ref, bg_ref,
                  q_ref, k_ref, v_ref, xc_ref, gates_ref, ext_ref,
                  *, tl, tiles_per_seq, kscale):
    i = pl.program_id(0)
    c = xm_ref.shape[1]
    first = lax.rem(i, tiles_per_seq) == 0

    @pl.when(first)
    def _():
        ext_ref[0:HALO, :] = jnp.zeros((HALO, c), F32)

    @pl.when(jnp.logical_not(first))
    def _():
        ext_ref[0:HALO, :] = ext_ref[tl:tl + HALO, :]

    xm = xm_ref[...]
    ext_ref[HALO:HALO + tl, :] = xm
    acc = jnp.broadcast_to(cb_ref[...], (tl, c))
    for j in range(A_CONV):
        off = HALO - (A_CONV - 1) + j
        acc = acc + cw_ref[j:j + 1, :] * ext_ref[off:off + tl, :]
    xc = _silu(acc)
    xcb = xc.astype(BF16)
    xmb = xm.astype(BF16)
    xc_ref[...] = xcb
    gates = jnp.broadcast_to(bg_ref[...], (tl, LANES))
    for s in range(c // MXU_DIM):
        sl = slice(MXU_DIM * s, MXU_DIM * (s + 1))
        qb = _dot(xcb[:, sl], bdq_ref[s]).astype(BF16)
        kf = _dot(xcb[:, sl], bdk_ref[s])
        kb = kf.astype(BF16)
        vb = _dot(xmb[:, sl], bdv_ref[s]).astype(BF16)
        gates = gates + _dot(qb, wgq_ref[sl, :]) + _dot(kb, wgk_ref[sl, :]) + _dot(vb, wgv_ref[sl, :])
        q_ref[:, sl] = qb
        k_ref[:, sl] = (kf * kscale).astype(BF16)
        v_ref[:, sl] = vb
    gates_ref[...] = gates


def _a_local(xm_src, seq, cw, cb, bdq, bdk, bdv, wgq, wgk, wgv, bg, kscale):
    t = xm_src.shape[0]
    c = cw.shape[1]
    tl = A_LOCAL_TILE
    nslab = c // MXU_DIM
    full = lambda shape: pl.BlockSpec(shape, lambda i: (0,) * len(shape))
    tok = lambda w: pl.BlockSpec((tl, w), lambda i: (i, 0))
    return pl.pallas_call(
        functools.partial(_a_local_body, tl=tl, tiles_per_seq=seq // tl, kscale=kscale),
        grid=(t // tl,),
        in_specs=[tok(c), full((A_CONV, c)), full((1, c)),
                  full((nslab, MXU_DIM, MXU_DIM)), full((nslab, MXU_DIM, MXU_DIM)),
                  full((nslab, MXU_DIM, MXU_DIM)),
                  full((c, LANES)), full((c, LANES)), full((c, LANES)), full((1, LANES))],
        out_specs=[tok(c), tok(c), tok(c), tok(c), tok(LANES)],
        out_shape=[jax.ShapeDtypeStruct((t, c), BF16)] * 4 + [jax.ShapeDtypeStruct((t, LANES), F32)],
        scratch_shapes=[pltpu.VMEM((tl + 2 * HALO, c), F32)],
        compiler_params=_params(1),
        name="a_local",
    )(xm_src, cw, cb, bdq, bdk, bdv, wgq, wgk, wgv, bg)


def _a_cell_body(q_ref, k_ref, v_ref, xc_ref, o_ref, z_ref, gates_ref, hn_ref, skip_ref,
                 y_ref, c_ref, n_ref, m_ref, *, chunk, dh):
    cidx = pl.program_id(1)
    nh = A_HEADS

    @pl.when(cidx == 0)
    def _():
        c_ref[...] = jnp.zeros(c_ref.shape, F32)
        n_ref[...] = jnp.zeros(n_ref.shape, F32)
        m_ref[...] = jnp.zeros(m_ref.shape, F32)

    gates = gates_ref[...]
    lf = jnp.minimum(gates, 0.0) - jnp.log(1.0 + jnp.exp(-jnp.abs(gates)))
    r_i = lax.broadcasted_iota(jnp.int32, (chunk, chunk), 0)
    c_i = lax.broadcasted_iota(jnp.int32, (chunk, chunk), 1)
    causal = r_i >= c_i
    tril = causal.astype(BF16)
    hi = lf.astype(BF16)
    r1 = lf - hi.astype(F32)
    mid = r1.astype(BF16)
    lo = (r1 - mid.astype(F32)).astype(BF16)
    b_col = _dot(tril, hi) + _dot(tril, mid) + _dot(tril, lo)
    g_t = gates.T
    b_t = b_col.T

    for h in range(nh):
        hs = slice(h * dh, (h + 1) * dh)
        qh = q_ref[:, hs]
        kh = k_ref[:, hs]
        vh = v_ref[:, hs]
        i_row = g_t[h:h + 1, :]
        b_row = b_t[nh + h:nh + h + 1, :]
        i_c = gates[:, h:h + 1]
        b_c = b_col[:, nh + h:nh + h + 1]
        m_b = m_ref[h:h + 1, 0:1]
        log_d = jnp.where(causal, b_c - b_row + i_row, -jnp.inf)
        m_inter = b_c + m_b
        m_t = jnp.maximum(jnp.max(log_d, axis=1, keepdims=True), m_inter)
        d = jnp.exp(log_d - m_t)
        s = _dot_nt(qh, kh) * d
        decay = jnp.exp(m_inter - m_t)
        inter = _dot(qh, c_ref[h].astype(BF16))
        intra = _dot(s.astype(BF16), vh)
        num = decay * inter + intra
        qn = jnp.sum(qh.astype(F32) * n_ref[h:h + 1, :], axis=1, keepdims=True)
        den = decay * qn + jnp.sum(s, axis=1, keepdims=True)
        hcell = num * (1.0 / jnp.maximum(jnp.abs(den), jnp.exp(-m_t)))
        b_last = b_col[chunk - 1:chunk, nh + h:nh + h + 1]
        m_new = jnp.maximum(b_last + m_b,
                            jnp.max(b_last - b_row + i_row, axis=1, keepdims=True))
        w_c = jnp.exp(b_last - b_c + i_c - m_new)
        carry = jnp.exp(b_last + m_b - m_new)
        wv = (vh.astype(F32) * w_c).astype(BF16)
        c_ref[h] = carry * c_ref[h] + _dot_tn(kh, wv)
        n_ref[h:h + 1, :] = carry * n_ref[h:h + 1, :] + jnp.sum(
            kh.astype(F32) * w_c, axis=0, keepdims=True)
        m_ref[h:h + 1, :] = jnp.broadcast_to(m_new, (1, LANES))
        hc = _sigmoid(o_ref[:, hs]) * hcell
        mu = jnp.mean(hc, axis=1, keepdims=True)
        cen = hc - mu
        var = jnp.mean(cen * cen, axis=1, keepdims=True)
        hn = cen * lax.rsqrt(var + EPS) * hn_ref[:, hs]
        y = (hn + skip_ref[:, hs] * xc_ref[:, hs].astype(F32)) * _silu(z_ref[:, hs])
        y_ref[:, hs] = y.astype(BF16)


def _a_cell(q, k, v, xc, proj, gates, head_norm, skip, batch, seq):
    t, c = q.shape
    chunk = A_CHUNK
    nc = seq // chunk
    dh = c // A_HEADS
    tokc = lambda j: pl.BlockSpec((chunk, c), lambda b, i, j=j: (b * nc + i, j))
    full = lambda shape: pl.BlockSpec(shape, lambda b, i: (0,) * len(shape))
    return pl.pallas_call(
        functools.partial(_a_cell_body, chunk=chunk, dh=dh),
        grid=(batch, nc),
        in_specs=[tokc(0), tokc(0), tokc(0), tokc(0), tokc(1), tokc(2),
                  pl.BlockSpec((chunk, LANES), lambda b, i: (b * nc + i, 0)),
                  full((1, c)), full((1, c))],
        out_specs=tokc(0),
        out_shape=jax.ShapeDtypeStruct((t, c), BF16),
        scratch_shapes=[pltpu.VMEM((A_HEADS, dh, dh), F32),
                        pltpu.VMEM((SUBLANES, dh), F32),
                        pltpu.VMEM((SUBLANES, LANES), F32)],
        compiler_params=_params(2),
        name="a_cell",
    )(q, k, v, xc, proj, proj, gates, head_norm, skip)


def _kv_compress_body(x_ref, pa_ref, pb_ref, w1a_ref, w1b_ref, b1_ref, w2_ref, b2_ref,
                      o_ref, nxt_ref):
    x = x_ref[0, 0]
    nr = x.shape[0]
    first = _dot((x + pa_ref[0]).astype(BF16), w1a_ref[0])
    nxt_ref[0:nr, :] = _dot((x + pb_ref[0]).astype(BF16), w1b_ref[0])
    nxt_ref[nr:nr + SUBLANES, :] = jnp.zeros((SUBLANES, nxt_ref.shape[1]), F32)
    h1 = _silu(first + nxt_ref[1:nr + 1, :] + b1_ref[0])
    o_ref[0, 0] = (_dot(h1.astype(BF16), w2_ref[0]) + b2_ref[0]).astype(o_ref.dtype)


def _kv_compress(x16, pos_a, pos_b, w1a, w1b, b1, w2, b2):
    nkg, batch, nr, half = x16.shape
    hid = w1a.shape[2]
    dk = w2.shape[2]
    kind = lambda shape: pl.BlockSpec((1,) + shape, lambda p, b: (p // B_GROUPS, 0, 0))
    return pl.pallas_call(
        _kv_compress_body,
        grid=(nkg, batch),
        in_specs=[pl.BlockSpec((1, 1, nr, half), lambda p, b: (p, b, 0, 0)),
                  kind((1, half)), kind((1, half)),
                  kind((half, hid)), kind((half, hid)), kind((1, hid)),
                  kind((hid, dk)), kind((1, dk))],
        out_specs=pl.BlockSpec((1, 1, nr, dk), lambda p, b: (p, b, 0, 0)),
        out_shape=jax.ShapeDtypeStruct((nkg, batch, nr, dk), BF16),
        scratch_shapes=[pltpu.VMEM((nr + SUBLANES, hid), F32)],
        compiler_params=_params(2),
        name="kv_compress",
    )(x16, pos_a, pos_b, w1a, w1b, b1, w2, b2)


def _stack_heads(q):
    return jnp.concatenate([q[:, h * B_DK:(h + 1) * B_DK] for h in range(B_HPG)], axis=0)


LOG2E = 1.4426950408889634


def _head_slopes(g):
    return [LOG2E * jnp.exp2(jnp.full((1, LANES), -0.5, F32) * (g * B_HPG + h + 1).astype(F32))
            for h in range(B_HPG)]


def _lane_tiles(a):
    return [a[:, j * LANES:(j + 1) * LANES] for j in range(a.shape[1] // LANES)]


def _head_logits(s, h, tq, slope, rel, madd):
    rows = slice(h * tq, (h + 1) * tq)
    return [s[rows, j * LANES:(j + 1) * LANES] + (slope * r + ma)
            for j, (r, ma) in enumerate(zip(_lane_tiles(rel), _lane_tiles(madd)))]


def _tile_max(tiles):
    mx = tiles[0]
    for t in tiles[1:]:
        mx = jnp.maximum(mx, t)
    return jnp.max(mx, axis=1, keepdims=True)


def _tile_sum(tiles):
    sm = tiles[0]
    for t in tiles[1:]:
        sm = sm + t
    return jnp.sum(sm, axis=1, keepdims=True)


def _nsa_cmp_body(q_ref, kc_ref, vc_ref, zg_ref, gl_ref, wov_ref,
                  oc_ref, sel_ref, any_ref, *, tq, n_sel):
    g = pl.program_id(1)
    t0 = pl.program_id(2) * tq
    qs = _stack_heads(q_ref[...])
    kc = kc_ref[0, 0]
    ncp = kc.shape[0]
    nsp = wov_ref.shape[0]
    slopes = _head_slopes(g)
    n = lax.broadcasted_iota(jnp.int32, (1, ncp), 1)
    rel = ((n * CMP_STRIDE - t0).astype(F32) + 0.5 * (CMP_BLK - 1))
    c_end = lax.broadcasted_iota(jnp.int32, (tq, ncp), 1) * CMP_STRIDE + (CMP_BLK - 1)
    tok = t0 + lax.broadcasted_iota(jnp.int32, (tq, ncp), 0)
    madd = jnp.where(c_end <= tok, 0.0, NEG)
    s = _dot_nt(qs, kc)
    p_rows = []
    psum_tiles = None
    for h in range(B_HPG):
        tiles = _head_logits(s, h, tq, slopes[h], rel, madd)
        m = jnp.maximum(_tile_max(tiles), M_INIT)
        es = [jnp.exp2(t - m) for t in tiles]
        inv = 1.0 / jnp.maximum(_tile_sum(es), 1e-30)
        ps = [e * inv for e in es]
        psum_tiles = ps if psum_tiles is None else [a + b_ for a, b_ in zip(psum_tiles, ps)]
        p_rows.append(jnp.concatenate([p.astype(BF16) for p in ps], axis=1))
    o = _dot(jnp.concatenate(p_rows, axis=0), vc_ref[0, 0])
    psum = jnp.concatenate(psum_tiles, axis=1)
    p_hi = psum.astype(BF16)
    p_lo = (psum - p_hi.astype(F32)).astype(BF16)
    wov = wov_ref[...]
    imp = _dot_nt(wov, p_hi) + _dot_nt(wov, p_lo)
    j = lax.broadcasted_iota(jnp.int32, (nsp, tq), 0)
    cur = _shr(t0 + lax.broadcasted_iota(jnp.int32, (1, tq), 1), SEL_BLK)
    forced = (j == 0) | (j == cur) | (j == cur - 1)
    val = jnp.where(forced, jnp.inf, jnp.where(j <= cur, imp, -jnp.inf))
    sel_t = jnp.zeros((nsp, tq), F32)
    for _ in range(n_sel):
        mx = jnp.max(val, axis=0, keepdims=True)
        cand = jnp.where((val == mx) & (mx > -jnp.inf), j, nsp)
        pick = j == jnp.min(cand, axis=0, keepdims=True)
        sel_t = jnp.where(pick, 1.0, sel_t)
        val = jnp.where(pick, -jnp.inf, val)
    sel = sel_t.T
    sel_ref[0, 0] = sel.astype(sel_ref.dtype)
    any_ref[0, 0, 0] = jnp.broadcast_to(jnp.max(sel, axis=0, keepdims=True), (SUBLANES, nsp))
    gate = _sigmoid(gl_ref[...])
    for h in range(B_HPG):
        hs = slice(h * B_DK, (h + 1) * B_DK)
        oc_ref[:, hs] = gate[:, h:h + 1] * _silu(zg_ref[:, hs]) * o[h * tq:(h + 1) * tq]


def _nsa_cmp(q, kvc, proj_r, wov_t, batch, seq, n_sel):
    t = q.shape[0]
    tq = Q_TILE
    nqb = seq // tq
    ncp = kvc.shape[2]
    nsp = wov_t.shape[0]
    gw = B_HPG * B_DK
    zoff = 0
    goff = (N_BRANCH * B_HEADS * B_DK) // LANES
    return pl.pallas_call(
        functools.partial(_nsa_cmp_body, tq=tq, n_sel=n_sel),
        grid=(batch, B_GROUPS, nqb),
        in_specs=[pl.BlockSpec((tq, gw), lambda b, g, i: (b * nqb + i, g)),
                  pl.BlockSpec((1, 1, ncp, B_DK), lambda b, g, i: (g, b, 0, 0)),
                  pl.BlockSpec((1, 1, ncp, B_DK), lambda b, g, i: (B_GROUPS + g, b, 0, 0)),
                  pl.BlockSpec((tq, gw), lambda b, g, i: (b * nqb + i, zoff + g)),
                  pl.BlockSpec((tq, LANES), lambda b, g, i: (b * nqb + i, goff + g)),
                  pl.BlockSpec((nsp, ncp), lambda b, g, i: (0, 0))],
        out_specs=[pl.BlockSpec((tq, gw), lambda b, g, i: (b * nqb + i, g)),
                   pl.BlockSpec((1, 1, tq, nsp), lambda b, g, i: (b, g, i, 0)),
                   pl.BlockSpec((1, 1, 1, SUBLANES, nsp), lambda b, g, i: (b, g, i, 0, 0))],
        out_shape=[jax.ShapeDtypeStruct((t, B_GROUPS * gw), F32),
                   jax.ShapeDtypeStruct((batch, B_GROUPS, seq, nsp), BF16),
                   jax.ShapeDtypeStruct((batch, B_GROUPS, nqb, SUBLANES, nsp), F32)],
        compiler_params=_params(3),
        name="nsa_cmp",
    )(q, kvc, kvc, proj_r, proj_r, wov_t)


def _online_update(qs, k_c, v_c, slopes, rel, madd, m_ref, l_ref, acc_ref, tq):
    s = _dot_nt(qs, k_c)
    p_rows = []
    alphas = []
    for h in range(B_HPG):
        rows = slice(h * tq, (h + 1) * tq)
        tiles = _head_logits(s, h, tq, slopes[h], rel, madd)
        m_prev = m_ref[rows, :]
        m_new = jnp.maximum(m_prev, _tile_max(tiles))
        ps = [jnp.exp2(t - m_new) for t in tiles]
        alpha = jnp.exp2(m_prev - m_new)
        l_ref[rows, :] = alpha * l_ref[rows, :] + _tile_sum(ps)
        m_ref[rows, :] = m_new
        alphas.append(alpha)
        p_rows.append(jnp.concatenate([p.astype(BF16) for p in ps], axis=1))
    pv = _dot(jnp.concatenate(p_rows, axis=0), v_c)
    for h in range(B_HPG):
        rows = slice(h * tq, (h + 1) * tq)
        acc_ref[rows, :] = alphas[h] * acc_ref[rows, :] + pv[rows]


def _softmax_reset(m_ref, l_ref, acc_ref):
    m_ref[...] = jnp.full(m_ref.shape, M_INIT, F32)
    l_ref[...] = jnp.zeros(l_ref.shape, F32)
    acc_ref[...] = jnp.zeros(acc_ref.shape, F32)


def _softmax_result(l_ref, acc_ref):
    return acc_ref[...] * (1.0 / jnp.maximum(l_ref[...], 1e-30))


def _nsa_sw_body(flags_ref, q_ref, sel_ref, ks_ref, vs_ref, kw_ref, vw_ref,
                 zs_ref, zw_ref, gl_ref, oc_ref, out_ref, m_ref, l_ref, acc_ref,
                 *, tq, nqb, nch):
    b = pl.program_id(0)
    g = pl.program_id(1)
    i = pl.program_id(2)
    t0 = i * tq
    qs = _stack_heads(q_ref[...])
    slopes = _head_slopes(g)
    sel = sel_ref[0, 0]
    nsp = sel.shape[1]
    ch = SEL_CHUNK
    bpc = ch // SEL_BLK
    fbase = ((b * B_GROUPS + g) * nqb + i) * nch

    def positions(start, width):
        rel = (start - t0) + lax.broadcasted_iota(jnp.int32, (1, width), 1)
        dist = lax.broadcasted_iota(jnp.int32, (tq, width), 0) - (
            (start - t0) + lax.broadcasted_iota(jnp.int32, (tq, width), 1))
        return rel.astype(F32), dist

    _softmax_reset(m_ref, l_ref, acc_ref)

    def sel_chunk(c, carry):
        @pl.when(flags_ref[fbase + c] > 0)
        def _():
            start = pl.multiple_of(c * ch, ch)
            jj = lax.broadcasted_iota(jnp.int32, (nsp, ch), 0)
            xx = lax.broadcasted_iota(jnp.int32, (nsp, ch), 1)
            expand = (jj == c * bpc + _shr(xx, SEL_BLK)).astype(BF16)
            chosen = _dot(sel, expand)
            rel, dist = positions(start, ch)
            madd = jnp.where(chosen > 0.5, jnp.where(dist >= 0, 0.0, NEG), NEG)
            _online_update(qs, ks_ref[0, 0, pl.ds(start, ch), :], vs_ref[0, 0, pl.ds(start, ch), :],
                           slopes, rel, madd, m_ref, l_ref, acc_ref, tq)
        return carry

    lax.fori_loop(0, _shr(t0 + tq + ch - 1, ch), sel_chunk, 0)
    o_s = _softmax_result(l_ref, acc_ref)

    _softmax_reset(m_ref, l_ref, acc_ref)
    c_hi = _shr(t0 + tq - 1, ch)
    for back in range((WINDOW + tq - 3) // ch + 1, -1, -1):
        c = c_hi - back

        @pl.when(jnp.logical_and(c >= 0, (c + 1) * ch > t0 - (WINDOW - 1)))
        def _(c=c):
            st = pl.multiple_of(c * ch, ch)
            rel, dist = positions(st, ch)
            madd = jnp.where(dist >= 0, jnp.where(dist < WINDOW, 0.0, NEG), NEG)
            _online_update(qs, kw_ref[0, 0, pl.ds(st, ch), :], vw_ref[0, 0, pl.ds(st, ch), :],
                           slopes, rel, madd, m_ref, l_ref, acc_ref, tq)

    o_w = _softmax_result(l_ref, acc_ref)

    gate = _sigmoid(gl_ref[...])
    for h in range(B_HPG):
        hs = slice(h * B_DK, (h + 1) * B_DK)
        rs = slice(h * tq, (h + 1) * tq)
        merged = (oc_ref[:, hs]
                  + gate[:, B_HPG + h:B_HPG + h + 1] * _silu(zs_ref[:, hs]) * o_s[rs]
                  + gate[:, 2 * B_HPG + h:2 * B_HPG + h + 1] * _silu(zw_ref[:, hs]) * o_w[rs])
        out_ref[:, hs] = merged.astype(out_ref.dtype)


def _nsa_sw(flags, q, sel, kv_att, proj_r, oc, batch, seq):
    t = q.shape[0]
    tq = Q_TILE
    nqb = seq // tq
    nch = seq // SEL_CHUNK
    nsp = sel.shape[3]
    gw = B_HPG * B_DK
    goff = (N_BRANCH * B_HEADS * B_DK) // LANES
    rows = B_HPG * tq
    tok = lambda j: pl.BlockSpec((tq, gw), lambda b, g, i, f, j=j: (b * nqb + i, j * B_GROUPS + g))
    res = lambda k: pl.BlockSpec((1, 1, seq, B_DK), lambda b, g, i, f, k=k: (k * B_GROUPS + g, b, 0, 0))
    grid_spec = pltpu.PrefetchScalarGridSpec(
        num_scalar_prefetch=1,
        grid=(batch, B_GROUPS, nqb),
        in_specs=[tok(0),
                  pl.BlockSpec((1, 1, tq, nsp), lambda b, g, i, f: (b, g, i, 0)),
                  res(0), res(1), res(2), res(3),
                  tok(1), tok(2),
                  pl.BlockSpec((tq, LANES), lambda b, g, i, f: (b * nqb + i, goff + g)),
                  tok(0)],
        out_specs=tok(0),
        scratch_shapes=[pltpu.VMEM((rows, LANES), F32),
                        pltpu.VMEM((rows, LANES), F32),
                        pltpu.VMEM((rows, B_DK), F32)])
    return pl.pallas_call(
        functools.partial(_nsa_sw_body, tq=tq, nqb=nqb, nch=nch),
        grid_spec=grid_spec,
        out_shape=jax.ShapeDtypeStruct((t, B_GROUPS * gw), BF16),
        compiler_params=_params(3),
        name="nsa_sw",
    )(flags, q, sel, kv_att, kv_att, kv_att, kv_att, proj_r, proj_r, proj_r, oc)


def _blockdiag_dense(w):
    nb, bs, _ = w.shape
    per = MXU_DIM // bs
    w4 = w.reshape(nb // per, per, bs, bs)
    dense = jnp.einsum("sgij,gh->sgihj", w4, jnp.eye(per, dtype=w.dtype))
    return dense.reshape(nb // per, MXU_DIM, MXU_DIM).astype(BF16)


def _pad_cols(w, width):
    return jnp.pad(w, ((0, 0), (0, width - w.shape[1])))


def _overlap_matrix_t(nr, nsp, nc, ns):
    i = jnp.arange(nr)[None, :] * CMP_STRIDE
    j = jnp.arange(nsp)[:, None] * SEL_BLK
    ov = jnp.minimum(i + CMP_BLK, j + SEL_BLK) - jnp.maximum(i, j)
    ov = jnp.maximum(ov, 0) / CMP_STRIDE
    live = (jnp.arange(nr)[None, :] < nc) & (jnp.arange(nsp)[:, None] < ns)
    return jnp.where(live, ov, 0).astype(BF16)


def _mlstm_layer(xf, batch, seq, g_pre, g_post, w_in, conv_w, conv_b, w_q, w_k, w_v,
                 w_gate, b_gate, head_norm, skip, w_out):
    c = conv_w.shape[1]
    dh = c // A_HEADS
    proj = _rms_matmul(xf, g_pre, w_in.astype(BF16), F32)
    wg = [_pad_cols(w_gate[j * c:(j + 1) * c], LANES).astype(BF16) for j in range(3)]
    bg = _pad_cols(b_gate.reshape(1, -1), LANES).astype(F32)
    q, k, v, xc, gates = _a_local(
        proj, seq, conv_w.astype(F32), conv_b.reshape(1, c).astype(F32),
        _blockdiag_dense(w_q), _blockdiag_dense(w_k), _blockdiag_dense(w_v),
        wg[0], wg[1], wg[2], bg, float(dh) ** -0.5)
    y = _a_cell(q, k, v, xc, proj, gates, head_norm.reshape(1, c).astype(F32),
                skip.reshape(1, c).astype(F32), batch, seq)
    return _out_proj(y, w_out.astype(BF16), g_post, xf)


def _shared_kv(xf, batch, seq, kv_norm, kv_w, cmp_pos, cmp_w1, cmp_b1, cmp_w2, cmp_b2):
    ncmp = 2 * B_GROUPS * B_DK
    kv_w = kv_w.astype(BF16)
    kv_cmp = _rms_matmul(xf, kv_norm, kv_w[:, :ncmp], F32, split=True)
    kv_att = _rms_matmul(xf, kv_norm, kv_w[:, ncmp:], BF16, split=True)
    nr = seq // CMP_STRIDE
    half = CMP_STRIDE * B_DK
    pos_a = cmp_pos[:, :CMP_STRIDE].reshape(2, 1, half).astype(F32)
    pos_b = cmp_pos[:, CMP_STRIDE:].reshape(2, 1, half).astype(F32)
    kvc = _kv_compress(kv_cmp.reshape(2 * B_GROUPS, batch, nr, half), pos_a, pos_b,
                       cmp_w1[:, :half].astype(BF16), cmp_w1[:, half:].astype(BF16),
                       cmp_b1.reshape(2, 1, -1).astype(F32), cmp_w2.astype(BF16),
                       cmp_b2.reshape(2, 1, -1).astype(F32))
    return kvc, kv_att.reshape(4 * B_GROUPS, batch, seq, B_DK)


def _nsa_layer(xf, batch, seq, g_pre, g_post, w_in, w_out, shared, wov_t, n_sel):
    kvc, kv_att = shared
    nqk = B_HEADS * B_DK
    ngl = N_BRANCH * B_HEADS
    w_q = w_in[:, :nqk].astype(BF16)
    w_gl = w_in[:, nqk:nqk + ngl].reshape(-1, N_BRANCH, B_GROUPS, B_HPG)
    w_gl = jnp.transpose(w_gl, (0, 2, 1, 3)).reshape(-1, B_GROUPS, N_BRANCH * B_HPG)
    w_gl = jnp.pad(w_gl, ((0, 0), (0, 0), (0, LANES - N_BRANCH * B_HPG))).reshape(-1, B_GROUPS * LANES)
    w_r = jnp.concatenate([w_in[:, nqk + ngl:], w_gl], axis=1).astype(BF16)
    q = _rms_matmul(xf, g_pre, w_q, BF16, scale=LOG2E * float(B_DK) ** -0.5)
    proj_r = _rms_matmul(xf, g_pre, w_r, F32)
    oc, sel, blk_any = _nsa_cmp(q, kvc, proj_r, wov_t, batch, seq, n_sel)
    ns = seq // SEL_BLK
    bpc = SEL_CHUNK // SEL_BLK
    flags = blk_any[:, :, :, 0, :ns].reshape(batch, B_GROUPS, seq // Q_TILE, ns // bpc, bpc)
    flags = (jnp.max(flags, axis=-1) > 0).astype(jnp.int32).reshape(-1)
    out = _nsa_sw(flags, q, sel, kv_att, proj_r, oc, batch, seq)
    return _out_proj(out, w_out.astype(BF16), g_post, xf)


def kernel(x, norm_pre, norm_post, a_w_in, a_conv_w, a_conv_b, a_w_q, a_w_k, a_w_v, a_w_gate,
           a_b_gate, a_head_norm, a_skip, a_w_out, kv_norm, kv_w, cmp_pos, cmp_w1, cmp_b1,
           cmp_w2, cmp_b2, b_w_in, b_w_out):
    batch, seq, d = x.shape
    na = a_w_in.shape[0]
    nb = b_w_in.shape[0]
    assert seq % (CMP_STRIDE * LANES) == 0 and seq % SEL_CHUNK == 0 and seq % A_CHUNK == 0
    xf = x.reshape(batch * seq, d).astype(F32)
    for l in range(na):
        xf = _mlstm_layer(xf, batch, seq, norm_pre[l], norm_post[l], a_w_in[l], a_conv_w[l],
                          a_conv_b[l], a_w_q[l], a_w_k[l], a_w_v[l], a_w_gate[l], a_b_gate[l],
                          a_head_norm[l], a_skip[l], a_w_out[l])
    shared = _shared_kv(xf, batch, seq, kv_norm, kv_w, cmp_pos, cmp_w1, cmp_b1, cmp_w2, cmp_b2)
    nr = seq // CMP_STRIDE
    ns = seq // SEL_BLK
    nsp = -(-ns // LANES) * LANES
    wov_t = _overlap_matrix_t(nr, nsp, nr - CMP_BLK // CMP_STRIDE + 1, ns)
    for l in range(nb):
        xf = _nsa_layer(xf, batch, seq, norm_pre[na + l], norm_post[na + l], b_w_in[l],
                        b_w_out[l], shared, wov_t, min(SEL_TOP, ns))
    return xf.reshape(batch, seq, d).astype(x.dtype)
```

```python
import functools

import jax
import jax.numpy as jnp
from jax import lax
from jax.experimental import pallas as pl
from jax.experimental.pallas import tpu as pltpu

F32 = jnp.float32
BF16 = jnp.bfloat16
EPS = 1e-6

A_HEADS = 4
A_CONV = 4
B_HEADS = 16
B_GROUPS = 2
B_HPG = B_HEADS // B_GROUPS
B_DK = 128
N_BRANCH = 3
CMP_BLK = 32
CMP_STRIDE = 16
CMP_HIDDEN = 256
SEL_BLK = 64
SEL_TOP = 16
WINDOW = 512

LANES = 128
SUBLANES = 8
MXU_DIM = 256
VMEM_LIMIT_BYTES = 56 * 1024 * 1024

A_CHUNK = 256
A_LOCAL_TILE = 256
Q_TILE = 128
SEL_CHUNK = 256
NEG = -1e30
M_INIT = -1e20
MASK_NEG = -2.0 ** 100
assert B_DK == LANES


def _params(n_axes):
    return pltpu.CompilerParams(
        dimension_semantics=("arbitrary",) * n_axes,
        vmem_limit_bytes=VMEM_LIMIT_BYTES)


def _sigmoid(v):
    return 1.0 / (1.0 + jnp.exp(-v))


def _silu(v):
    return v * _sigmoid(v)


def _shr(v, pow2):
    assert pow2 & (pow2 - 1) == 0
    return lax.shift_right_logical(v, jnp.int32(pow2.bit_length() - 1))


def _dot(a, b):
    return jnp.dot(a, b, preferred_element_type=F32)


def _dot_nt(a, b):
    return lax.dot_general(a, b, (((1,), (1,)), ((), ())), preferred_element_type=F32)


def _dot_tn(a, b):
    return lax.dot_general(a, b, (((0,), (0,)), ((), ())), preferred_element_type=F32)


def _rms_matmul_body(x_ref, g_ref, w_ref, o_ref, *, scale):
    x = x_ref[...]
    ms = jnp.mean(x * x, axis=-1, keepdims=True)
    h = (x * lax.rsqrt(ms + EPS)) * g_ref[...]
    acc = _dot(h.astype(BF16), w_ref[...])
    if scale != 1.0:
        acc = acc * scale
    if len(o_ref.shape) == 2:
        o_ref[...] = acc.astype(o_ref.dtype)
    else:
        for j in range(o_ref.shape[0]):
            o_ref[j] = acc[:, j * LANES:(j + 1) * LANES].astype(o_ref.dtype)


def _pick_tile(n, candidates):
    for c in candidates:
        if n % c == 0:
            return c
    return n


def _rms_matmul(x, g, w, out_dtype, scale=1.0, split=False):
    t, d = x.shape
    n = w.shape[1]
    tm = _pick_tile(t, (1024, 512, 256, 128))
    tn = _pick_tile(n, (2048, 1536, 1280, 1024, 896, 768, 512, 256, 128))
    if split:
        out_spec = pl.BlockSpec((tn // LANES, tm, LANES), lambda j, i: (j, i, 0))
        out_shape = jax.ShapeDtypeStruct((n // LANES, t, LANES), out_dtype)
    else:
        out_spec = pl.BlockSpec((tm, tn), lambda j, i: (i, j))
        out_shape = jax.ShapeDtypeStruct((t, n), out_dtype)
    return pl.pallas_call(
        functools.partial(_rms_matmul_body, scale=scale),
        grid=(n // tn, t // tm),
        in_specs=[pl.BlockSpec((tm, d), lambda j, i: (i, 0)),
                  pl.BlockSpec((1, d), lambda j, i: (0, 0)),
                  pl.BlockSpec((d, tn), lambda j, i: (0, j))],
        out_specs=out_spec,
        out_shape=out_shape,
        compiler_params=_params(2),
        name="rms_matmul",
    )(x, g.reshape(1, d).astype(F32), w)


def _out_proj_body(y_ref, w_ref, g_ref, x_ref, o_ref):
    r = _dot(y_ref[...], w_ref[...])
    ms = jnp.mean(r * r, axis=-1, keepdims=True)
    o_ref[...] = x_ref[...] + (r * lax.rsqrt(ms + EPS)) * g_ref[...]


def _out_proj(y, w, g, x):
    t, k = y.shape
    d = w.shape[1]
    tm = _pick_tile(t, (512, 256, 128))
    return pl.pallas_call(
        _out_proj_body,
        grid=(t // tm,),
        in_specs=[pl.BlockSpec((tm, k), lambda i: (i, 0)),
                  pl.BlockSpec((k, d), lambda i: (0, 0)),
                  pl.BlockSpec((1, d), lambda i: (0, 0)),
                  pl.BlockSpec((tm, d), lambda i: (i, 0))],
        out_specs=pl.BlockSpec((tm, d), lambda i: (i, 0)),
        out_shape=jax.ShapeDtypeStruct((t, d), F32),
        compiler_params=_params(1),
        name="out_proj",
    )(y, w, g.reshape(1, d).astype(F32), x)


HALO = SUBLANES


def _a_local_body(xm_ref, cw_ref, cb_ref, bdq_ref, bdk_ref, bdv_ref,
                  wgq_ref, wgk_ref, wgv_ref, bg_ref,
                  q_ref, k_ref, v_ref, xc_ref, gates_ref, ext_ref,
                  *, tl, tiles_per_seq, kscale):
    i = pl.program_id(0)
    c = xm_ref.shape[1]
    first = lax.rem(i, tiles_per_seq) == 0

    @pl.when(first)
    def _():
        ext_ref[0:HALO, :] = jnp.zeros((HALO, c), F32)

    @pl.when(jnp.logical_not(first))
    def _():
        ext_ref[0:HALO, :] = ext_ref[tl:tl + HALO, :]

    xm = xm_ref[...]
    ext_ref[HALO:HALO + tl, :] = xm
    acc = jnp.broadcast_to(cb_ref[...], (tl, c))
    for j in range(A_CONV):
        off = HALO - (A_CONV - 1) + j
        acc = acc + cw_ref[j:j + 1, :] * ext_ref[off:off + tl, :]
    xc = _silu(acc)
    xcb = xc.astype(BF16)
    xmb = xm.astype(BF16)
    xc_ref[...] = xcb
    gates = jnp.broadcast_to(bg_ref[...], (tl, LANES))
    for s in range(c // MXU_DIM):
        sl = slice(MXU_DIM * s, MXU_DIM * (s + 1))
        qb = _dot(xcb[:, sl], bdq_ref[s]).astype(BF16)
        kf = _dot(xcb[:, sl], bdk_ref[s])
        kb = kf.astype(BF16)
        vb = _dot(xmb[:, sl], bdv_ref[s]).astype(BF16)
        gates = gates + _dot(qb, wgq_ref[sl, :]) + _dot(kb, wgk_ref[sl, :]) + _dot(vb, wgv_ref[sl, :])
        q_ref[:, sl] = qb
        k_ref[:, sl] = (kf * kscale).astype(BF16)
        v_ref[:, sl] = vb
    gates_ref[...] = gates


def _a_local(xm_src, seq, cw, cb, bdq, bdk, bdv, wgq, wgk, wgv, bg, kscale):
    t = xm_src.shape[0]
    c = cw.shape[1]
    tl = A_LOCAL_TILE
    nslab = c // MXU_DIM
    full = lambda shape: pl.BlockSpec(shape, lambda i: (0,) * len(shape))
    tok = lambda w: pl.BlockSpec((tl, w), lambda i: (i, 0))
    return pl.pallas_call(
        functools.partial(_a_local_body, tl=tl, tiles_per_seq=seq // tl, kscale=kscale),
        grid=(t // tl,),
        in_specs=[tok(c), full((A_CONV, c)), full((1, c)),
                  full((nslab, MXU_DIM, MXU_DIM)), full((nslab, MXU_DIM, MXU_DIM)),
                  full((nslab, MXU_DIM, MXU_DIM)),
                  full((c, LANES)), full((c, LANES)), full((c, LANES)), full((1, LANES))],
        out_specs=[tok(c), tok(c), tok(c), tok(c), tok(LANES)],
        out_shape=[jax.ShapeDtypeStruct((t, c), BF16)] * 4 + [jax.ShapeDtypeStruct((t, LANES), F32)],
        scratch_shapes=[pltpu.VMEM((tl + 2 * HALO, c), F32)],
        compiler_params=_params(1),
        name="a_local",
    )(xm_src, cw, cb, bdq, bdk, bdv, wgq, wgk, wgv, bg)


def _a_cell_body(q_ref, k_ref, v_ref, xc_ref, o_ref, z_ref, gates_ref, hn_ref, skip_ref,
                 y_ref, c_ref, m_ref, *, chunk, dh):
    cidx = pl.program_id(1)
    nh = A_HEADS

    @pl.when(cidx == 0)
    def _():
        c_ref[...] = jnp.zeros(c_ref.shape, F32)
        m_ref[...] = jnp.zeros(m_ref.shape, F32)

    gates = gates_ref[...]
    lf = jnp.minimum(gates, 0.0) - jnp.log(1.0 + jnp.exp(-jnp.abs(gates)))
    r_i = lax.broadcasted_iota(jnp.int32, (chunk, chunk), 0)
    c_i = lax.broadcasted_iota(jnp.int32, (chunk, chunk), 1)
    causal = r_i >= c_i
    tril = causal.astype(BF16)
    hi = lf.astype(BF16)
    r1 = lf - hi.astype(F32)
    mid = r1.astype(BF16)
    lo = (r1 - mid.astype(F32)).astype(BF16)
    b_col = _dot(tril, hi) + _dot(tril, mid) + _dot(tril, lo)
    g_t = gates.T
    b_t = b_col.T

    for h in range(nh):
        hs = slice(h * dh, (h + 1) * dh)
        qh = q_ref[:, hs]
        kh = k_ref[:, hs]
        vh = v_ref[:, hs]
        i_row = g_t[h:h + 1, :]
        b_row = b_t[nh + h:nh + h + 1, :]
        i_c = gates[:, h:h + 1]
        b_c = b_col[:, nh + h:nh + h + 1]
        m_b = m_ref[h:h + 1, 0:1]
        log_d = jnp.where(causal, b_c - b_row + i_row, -jnp.inf)
        m_inter = b_c + m_b
        m_t = jnp.maximum(jnp.max(log_d, axis=1, keepdims=True), m_inter)
        d = jnp.exp(log_d - m_t)
        s = _dot_nt(qh, kh) * d
        decay = jnp.exp(m_inter - m_t)
        inter = _dot(qh, c_ref[h].astype(BF16))
        intra = _dot(s.astype(BF16), vh)
        num = decay * inter[:, :dh] + intra
        den = decay * inter[:, dh:] + jnp.sum(s, axis=1, keepdims=True)
        scale = 1.0 / jnp.maximum(jnp.abs(den), jnp.exp(-m_t))
        hcell = num * jnp.concatenate([scale] * (dh // LANES), axis=1)
        b_last = b_col[chunk - 1:chunk, nh + h:nh + h + 1]
        m_new = jnp.maximum(b_last + m_b,
                            jnp.max(b_last - b_row + i_row, axis=1, keepdims=True))
        w_c = jnp.exp(b_last - b_c + i_c - m_new)
        carry = jnp.exp(b_last + m_b - m_new)
        wv = jnp.concatenate([vh.astype(F32) * w_c, jnp.broadcast_to(w_c, (chunk, LANES))],
                             axis=1).astype(BF16)
        c_ref[h] = carry * c_ref[h] + _dot_tn(kh, wv)
        m_ref[h:h + 1, :] = jnp.broadcast_to(m_new, (1, LANES))
        hc = _sigmoid(o_ref[:, hs]) * hcell
        mu = jnp.mean(hc, axis=1, keepdims=True)
        cen = hc - mu
        var = jnp.mean(cen * cen, axis=1, keepdims=True)
        hn = cen * lax.rsqrt(var + EPS) * hn_ref[:, hs]
        y = (hn + skip_ref[:, hs] * xc_ref[:, hs].astype(F32)) * _silu(z_ref[:, hs])
        y_ref[:, hs] = y.astype(BF16)


def _a_cell(q, k, v, xc, proj, gates, head_norm, skip, batch, seq):
    t, c = q.shape
    chunk = A_CHUNK
    nc = seq // chunk
    dh = c // A_HEADS
    tokc = lambda j: pl.BlockSpec((chunk, c), lambda b, i, j=j: (b * nc + i, j))
    full = lambda shape: pl.BlockSpec(shape, lambda b, i: (0,) * len(shape))
    return pl.pallas_call(
        functools.partial(_a_cell_body, chunk=chunk, dh=dh),
        grid=(batch, nc),
        in_specs=[tokc(0), tokc(0), tokc(0), tokc(0), tokc(1), tokc(2),
                  pl.BlockSpec((chunk, LANES), lambda b, i: (b * nc + i, 0)),
                  full((1, c)), full((1, c))],
        out_specs=tokc(0),
        out_shape=jax.ShapeDtypeStruct((t, c), BF16),
        scratch_shapes=[pltpu.VMEM((A_HEADS, dh, dh + LANES), F32),
                        pltpu.VMEM((SUBLANES, LANES), F32)],
        compiler_params=_params(2),
        name="a_cell",
    )(q, k, v, xc, proj, proj, gates, head_norm, skip)


def _kv_compress_body(x_ref, pa_ref, pb_ref, w1a_ref, w1b_ref, b1_ref, w2_ref, b2_ref,
                      o_ref, nxt_ref):
    x = x_ref[0, 0]
    nr = x.shape[0]
    first = _dot((x + pa_ref[0]).astype(BF16), w1a_ref[0])
    nxt_ref[0:nr, :] = _dot((x + pb_ref[0]).astype(BF16), w1b_ref[0])
    nxt_ref[nr:nr + SUBLANES, :] = jnp.zeros((SUBLANES, nxt_ref.shape[1]), F32)
    h1 = _silu(first + nxt_ref[1:nr + 1, :] + b1_ref[0])
    o_ref[0, 0] = (_dot(h1.astype(BF16), w2_ref[0]) + b2_ref[0]).astype(o_ref.dtype)


def _kv_compress(x16, pos_a, pos_b, w1a, w1b, b1, w2, b2):
    nkg, batch, nr, half = x16.shape
    hid = w1a.shape[2]
    dk = w2.shape[2]
    kind = lambda shape: pl.BlockSpec((1,) + shape, lambda p, b: (p // B_GROUPS, 0, 0))
    return pl.pallas_call(
        _kv_compress_body,
        grid=(nkg, batch),
        in_specs=[pl.BlockSpec((1, 1, nr, half), lambda p, b: (p, b, 0, 0)),
                  kind((1, half)), kind((1, half)),
                  kind((half, hid)), kind((half, hid)), kind((1, hid)),
                  kind((hid, dk)), kind((1, dk))],
        out_specs=pl.BlockSpec((1, 1, nr, dk), lambda p, b: (p, b, 0, 0)),
        out_shape=jax.ShapeDtypeStruct((nkg, batch, nr, dk), BF16),
        scratch_shapes=[pltpu.VMEM((nr + SUBLANES, hid), F32)],
        compiler_params=_params(2),
        name="kv_compress",
    )(x16, pos_a, pos_b, w1a, w1b, b1, w2, b2)


def _stack_heads(q):
    return jnp.concatenate([q[:, h * B_DK:(h + 1) * B_DK] for h in range(B_HPG)], axis=0)


LOG2E = 1.4426950408889634


def _head_slopes(g):
    return [LOG2E * jnp.exp2(jnp.full((1, LANES), -0.5, F32) * (g * B_HPG + h + 1).astype(F32))
            for h in range(B_HPG)]


def _lane_tiles(a):
    return [a[:, j * LANES:(j + 1) * LANES] for j in range(a.shape[1] // LANES)]


def _head_logits(s, h, tq, slope, rel, madd):
    rows = slice(h * tq, (h + 1) * tq)
    bias = [slope * r for r in _lane_tiles(rel)]
    if madd is not None:
        bias = [b_ + ma for b_, ma in zip(bias, _lane_tiles(madd))]
    return [s[rows, j * LANES:(j + 1) * LANES] + b_ for j, b_ in enumerate(bias)]


def _with_ones(v):
    return jnp.concatenate([v, jnp.ones_like(v)], axis=1)


def _tile_max(tiles):
    mx = tiles[0]
    for t in tiles[1:]:
        mx = jnp.maximum(mx, t)
    return jnp.max(mx, axis=1, keepdims=True)


def _tile_sum(tiles):
    sm = tiles[0]
    for t in tiles[1:]:
        sm = sm + t
    return jnp.sum(sm, axis=1, keepdims=True)


def _nsa_cmp_body(q_ref, kc_ref, vc_ref, zg_ref, gl_ref, wov_ref,
                  oc_ref, sel_ref, any_ref, o_scr, imp_scr, *, tq, n_sel):
    g = pl.program_id(1)
    t0 = pl.program_id(2) * tq
    ncp = kc_ref.shape[2]
    nsp = wov_ref.shape[0]

    def attend(width):
        qs = _stack_heads(q_ref[...])
        slopes = _head_slopes(g)
        n = lax.broadcasted_iota(jnp.int32, (1, width), 1)
        rel = ((n * CMP_STRIDE - t0).astype(F32) + 0.5 * (CMP_BLK - 1))
        c_end = lax.broadcasted_iota(jnp.int32, (tq, width), 1) * CMP_STRIDE + (CMP_BLK - 1)
        tok = t0 + lax.broadcasted_iota(jnp.int32, (tq, width), 0)
        madd = jnp.where(c_end <= tok, 0.0, NEG)
        s = _dot_nt(qs, kc_ref[0, 0, 0:width, :])
        p_rows = []
        psum_tiles = None
        for h in range(B_HPG):
            tiles = _head_logits(s, h, tq, slopes[h], rel, madd)
            m = jnp.maximum(_tile_max(tiles), M_INIT)
            es = [jnp.exp2(t - m) for t in tiles]
            inv = 1.0 / jnp.maximum(_tile_sum(es), 1e-30)
            ps = [e * inv for e in es]
            psum_tiles = ps if psum_tiles is None else [a + b_ for a, b_ in zip(psum_tiles, ps)]
            p_rows.append(jnp.concatenate([p.astype(BF16) for p in ps], axis=1))
        o_scr[...] = _dot(jnp.concatenate(p_rows, axis=0), vc_ref[0, 0, 0:width, :])
        psum = jnp.concatenate(psum_tiles, axis=1)
        p_hi = psum.astype(BF16)
        p_lo = (psum - p_hi.astype(F32)).astype(BF16)
        wov = wov_ref[:, 0:width]
        imp_scr[...] = _dot_nt(wov, p_hi) + _dot_nt(wov, p_lo)

    ntile = ncp // LANES
    need = _shr(t0 + tq - 1, CMP_STRIDE * LANES) + 1
    for k in range(1, ntile + 1):
        @pl.when((need == k) if k < ntile else (need >= k))
        def _(k=k):
            attend(k * LANES)

    o = o_scr[...]
    imp = imp_scr[...]
    j = lax.broadcasted_iota(jnp.int32, (nsp, tq), 0)
    cur = _shr(t0 + lax.broadcasted_iota(jnp.int32, (1, tq), 1), SEL_BLK)
    forced = (j == 0) | (j == cur) | (j == cur - 1)
    val = jnp.where(forced, jnp.inf, jnp.where(j <= cur, imp, -jnp.inf))
    sel_t = jnp.zeros((nsp, tq), F32)
    for _ in range(n_sel):
        mx = jnp.max(val, axis=0, keepdims=True)
        cand = jnp.where((val == mx) & (mx > -jnp.inf), j, nsp)
        pick = j == jnp.min(cand, axis=0, keepdims=True)
        sel_t = jnp.where(pick, 1.0, sel_t)
        val = jnp.where(pick, -jnp.inf, val)
    sel = sel_t.T
    sel_ref[0, 0] = sel.astype(sel_ref.dtype)
    any_ref[0, 0, 0] = jnp.broadcast_to(jnp.max(sel, axis=0, keepdims=True), (SUBLANES, nsp))
    gate = _sigmoid(gl_ref[...])
    for h in range(B_HPG):
        hs = slice(h * B_DK, (h + 1) * B_DK)
        oc_ref[:, hs] = gate[:, h:h + 1] * _silu(zg_ref[:, hs]) * o[h * tq:(h + 1) * tq]


def _nsa_cmp(q, kvc, proj_r, wov_t, batch, seq, n_sel):
    t = q.shape[0]
    tq = Q_TILE
    nqb = seq // tq
    ncp = kvc.shape[2]
    nsp = wov_t.shape[0]
    gw = B_HPG * B_DK
    zoff = 0
    goff = (N_BRANCH * B_HEADS * B_DK) // LANES
    return pl.pallas_call(
        functools.partial(_nsa_cmp_body, tq=tq, n_sel=n_sel),
        grid=(batch, B_GROUPS, nqb),
        in_specs=[pl.BlockSpec((tq, gw), lambda b, g, i: (b * nqb + i, g)),
                  pl.BlockSpec((1, 1, ncp, B_DK), lambda b, g, i: (g, b, 0, 0)),
                  pl.BlockSpec((1, 1, ncp, B_DK), lambda b, g, i: (B_GROUPS + g, b, 0, 0)),
                  pl.BlockSpec((tq, gw), lambda b, g, i: (b * nqb + i, zoff + g)),
                  pl.BlockSpec((tq, LANES), lambda b, g, i: (b * nqb + i, goff + g)),
                  pl.BlockSpec((nsp, ncp), lambda b, g, i: (0, 0))],
        out_specs=[pl.BlockSpec((tq, gw), lambda b, g, i: (b * nqb + i, g)),
                   pl.BlockSpec((1, 1, tq, nsp), lambda b, g, i: (b, g, i, 0)),
                   pl.BlockSpec((1, 1, 1, SUBLANES, nsp), lambda b, g, i: (b, g, i, 0, 0))],
        out_shape=[jax.ShapeDtypeStruct((t, B_GROUPS * gw), F32),
                   jax.ShapeDtypeStruct((batch, B_GROUPS, seq, nsp), BF16),
                   jax.ShapeDtypeStruct((batch, B_GROUPS, nqb, SUBLANES, nsp), F32)],
        scratch_shapes=[pltpu.VMEM((B_HPG * tq, B_DK), F32), pltpu.VMEM((nsp, tq), F32)],
        compiler_params=_params(3),
        name="nsa_cmp",
    )(q, kvc, kvc, proj_r, proj_r, wov_t)


def _softmax_once(s, v, slopes, rel, madd, tq):
    p_rows = []
    for h in range(B_HPG):
        tiles = _head_logits(s, h, tq, slopes[h], rel, madd)
        m = jnp.maximum(_tile_max(tiles), M_INIT)
        p_rows.append(jnp.concatenate([jnp.exp2(t - m).astype(BF16) for t in tiles], axis=1))
    pv = _dot(jnp.concatenate(p_rows, axis=0), _with_ones(v))
    return pv[:, :B_DK] * (1.0 / jnp.maximum(pv[:, B_DK:], 1e-30))


def _online_update(s, v_c, slopes, rel, madd, m_ref, l_ref, acc_ref, tq):
    p_rows = []
    alphas = []
    m_news = []
    for h in range(B_HPG):
        rows = slice(h * tq, (h + 1) * tq)
        m_prev = m_ref[rows, :]
        m_new = jnp.maximum(m_prev, _tile_max(_head_logits(s, h, tq, slopes[h], rel, madd)))
        m_ref[rows, :] = m_new
        alphas.append(jnp.exp2(m_prev - m_new))
        m_news.append(m_new)
    for h in range(B_HPG):
        tiles = _head_logits(s, h, tq, slopes[h], rel, madd)
        p_rows.append(jnp.concatenate([jnp.exp2(t - m_news[h]).astype(BF16) for t in tiles], axis=1))
    pv = _dot(jnp.concatenate(p_rows, axis=0), _with_ones(v_c))
    for h in range(B_HPG):
        rows = slice(h * tq, (h + 1) * tq)
        acc_ref[rows, :] = alphas[h] * acc_ref[rows, :] + pv[rows, :B_DK]
        l_ref[rows, :] = alphas[h] * l_ref[rows, :] + pv[rows, B_DK:]


def _softmax_reset(m_ref, l_ref, acc_ref):
    m_ref[...] = jnp.full(m_ref.shape, M_INIT, F32)
    l_ref[...] = jnp.zeros(l_ref.shape, F32)
    acc_ref[...] = jnp.zeros(acc_ref.shape, F32)


def _softmax_result(l_ref, acc_ref):
    return acc_ref[...] * (1.0 / jnp.maximum(l_ref[...], 1e-30))


def _nsa_sw_body(lst_ref, q_ref, sel_ref, ks_ref, vs_ref, kw_ref, vw_ref,
                 zs_ref, zw_ref, gl_ref, oc_ref, out_ref,
                 qs_ref, s_a, s_b, m_ref, l_ref, acc_ref,
                 *, tq, nqb, nch):
    b = pl.program_id(0)
    g = pl.program_id(1)
    i = pl.program_id(2)
    t0 = i * tq
    qs_ref[:, :B_DK] = _stack_heads(q_ref[...])
    not_sel = (1.0 - sel_ref[0, 0].astype(F32)).astype(BF16)
    qs_ref[:, B_DK:] = jnp.concatenate([not_sel] * B_HPG, axis=0)
    slopes = _head_slopes(g)
    nsp = sel_ref.shape[3]
    ch = SEL_CHUNK
    bpc = ch // SEL_BLK
    lbase = ((b * B_GROUPS + g) * nqb + i) * (nch + 2)
    n_reg = lst_ref[lbase] - 1

    def positions(start, width):
        rel = (start - t0) + lax.broadcasted_iota(jnp.int32, (1, width), 1)
        dist = lax.broadcasted_iota(jnp.int32, (tq, width), 0) - (
            (start - t0) + lax.broadcasted_iota(jnp.int32, (tq, width), 1))
        return rel.astype(F32), dist

    def scores(c):
        start = pl.multiple_of(c * ch, ch)
        key_blk = c * bpc + _shr(lax.broadcasted_iota(jnp.int32, (ch, nsp), 0), SEL_BLK)
        own_blk = jnp.where(lax.broadcasted_iota(jnp.int32, (ch, nsp), 1) == key_blk, MASK_NEG, 0.0)
        keys = jnp.concatenate([ks_ref[0, 0, pl.ds(start, ch), :], own_blk.astype(BF16)], axis=1)
        return _dot_nt(qs_ref[...], keys)

    def stage(j, s_ref):
        s_ref[...] = scores(lst_ref[lbase + 1 + j])

    def consume(j, s_ref):
        start = pl.multiple_of(lst_ref[lbase + 1 + j] * ch, ch)
        rel, _ = positions(start, ch)
        _online_update(s_ref, vs_ref[0, 0, pl.ds(start, ch), :], slopes, rel, None,
                       m_ref, l_ref, acc_ref, tq)

    _softmax_reset(m_ref, l_ref, acc_ref)

    @pl.when(n_reg > 0)
    def _():
        stage(0, s_a)

    def chunk_pair(jj, carry):
        j = 2 * jj
        stage(j + 1, s_b)
        consume(j, s_a)

        @pl.when(j + 1 < n_reg)
        def _():
            stage(j + 2, s_a)
            consume(j + 1, s_b)
        return carry

    lax.fori_loop(0, _shr(n_reg + 1, 2), chunk_pair, 0)

    c_own = lst_ref[lbase + 1 + n_reg]
    start = pl.multiple_of(c_own * ch, ch)
    rel, dist = positions(start, ch)
    _online_update(scores(c_own), vs_ref[0, 0, pl.ds(start, ch), :], slopes, rel,
                   jnp.where(dist >= 0, 0.0, NEG), m_ref, l_ref, acc_ref, tq)
    o_s = _softmax_result(l_ref, acc_ref)

    nwin = WINDOW // ch + 1
    c_lo = jnp.maximum(_shr(t0 + tq - 1, ch) - (nwin - 1), 0)
    st = pl.multiple_of(c_lo * ch, ch)
    rel, dist = positions(st, nwin * ch)
    madd = jnp.where(dist >= 0, jnp.where(dist < WINDOW, 0.0, NEG), NEG)
    o_w = _softmax_once(_dot_nt(qs_ref[:, :B_DK], kw_ref[0, 0, pl.ds(st, nwin * ch), :]),
                        vw_ref[0, 0, pl.ds(st, nwin * ch), :], slopes, rel, madd, tq)

    gate = _sigmoid(gl_ref[...])
    for h in range(B_HPG):
        hs = slice(h * B_DK, (h + 1) * B_DK)
        rs = slice(h * tq, (h + 1) * tq)
        merged = (oc_ref[:, hs]
                  + gate[:, B_HPG + h:B_HPG + h + 1] * _silu(zs_ref[:, hs]) * o_s[rs]
                  + gate[:, 2 * B_HPG + h:2 * B_HPG + h + 1] * _silu(zw_ref[:, hs]) * o_w[rs])
        out_ref[:, hs] = merged.astype(out_ref.dtype)


def _nsa_sw(flags, q, sel, kv_att, proj_r, oc, batch, seq):
    t = q.shape[0]
    tq = Q_TILE
    nqb = seq // tq
    nch = seq // SEL_CHUNK
    nsp = sel.shape[3]
    gw = B_HPG * B_DK
    goff = (N_BRANCH * B_HEADS * B_DK) // LANES
    rows = B_HPG * tq
    tok = lambda j: pl.BlockSpec((tq, gw), lambda b, g, i, f, j=j: (b * nqb + i, j * B_GROUPS + g))
    res = lambda k: pl.BlockSpec((1, 1, seq, B_DK), lambda b, g, i, f, k=k: (k * B_GROUPS + g, b, 0, 0))
    grid_spec = pltpu.PrefetchScalarGridSpec(
        num_scalar_prefetch=1,
        grid=(batch, B_GROUPS, nqb),
        in_specs=[tok(0),
                  pl.BlockSpec((1, 1, tq, nsp), lambda b, g, i, f: (b, g, i, 0)),
                  res(0), res(1), res(2), res(3),
                  tok(1), tok(2),
                  pl.BlockSpec((tq, LANES), lambda b, g, i, f: (b * nqb + i, goff + g)),
                  tok(0)],
        out_specs=tok(0),
        scratch_shapes=[pltpu.VMEM((rows, B_DK + nsp), BF16),
                        pltpu.VMEM((rows, SEL_CHUNK), F32),
                        pltpu.VMEM((rows, SEL_CHUNK), F32),
                        pltpu.VMEM((rows, LANES), F32),
                        pltpu.VMEM((rows, LANES), F32),
                        pltpu.VMEM((rows, B_DK), F32)])
    return pl.pallas_call(
        functools.partial(_nsa_sw_body, tq=tq, nqb=nqb, nch=nch),
        grid_spec=grid_spec,
        out_shape=jax.ShapeDtypeStruct((t, B_GROUPS * gw), BF16),
        compiler_params=_params(3),
        name="nsa_sw",
    )(flags, q, sel, kv_att, kv_att, kv_att, kv_att, proj_r, proj_r, proj_r, oc)


def _blockdiag_dense(w):
    nb, bs, _ = w.shape
    per = MXU_DIM // bs
    w4 = w.reshape(nb // per, per, bs, bs)
    dense = jnp.einsum("sgij,gh->sgihj", w4, jnp.eye(per, dtype=w.dtype))
    return dense.reshape(nb // per, MXU_DIM, MXU_DIM).astype(BF16)


def _pad_cols(w, width):
    return jnp.pad(w, ((0, 0), (0, width - w.shape[1])))


def _overlap_matrix_t(nr, nsp, nc, ns):
    i = jnp.arange(nr)[None, :] * CMP_STRIDE
    j = jnp.arange(nsp)[:, None] * SEL_BLK
    ov = jnp.minimum(i + CMP_BLK, j + SEL_BLK) - jnp.maximum(i, j)
    ov = jnp.maximum(ov, 0) / CMP_STRIDE
    live = (jnp.arange(nr)[None, :] < nc) & (jnp.arange(nsp)[:, None] < ns)
    return jnp.where(live, ov, 0).astype(BF16)


def _mlstm_layer(xf, batch, seq, g_pre, g_post, w_in, conv_w, conv_b, w_q, w_k, w_v,
                 w_gate, b_gate, head_norm, skip, w_out):
    c = conv_w.shape[1]
    dh = c // A_HEADS
    proj = _rms_matmul(xf, g_pre, w_in.astype(BF16), F32)
    wg = [_pad_cols(w_gate[j * c:(j + 1) * c], LANES).astype(BF16) for j in range(3)]
    bg = _pad_cols(b_gate.reshape(1, -1), LANES).astype(F32)
    q, k, v, xc, gates = _a_local(
        proj, seq, conv_w.astype(F32), conv_b.reshape(1, c).astype(F32),
        _blockdiag_dense(w_q), _blockdiag_dense(w_k), _blockdiag_dense(w_v),
        wg[0], wg[1], wg[2], bg, float(dh) ** -0.5)
    y = _a_cell(q, k, v, xc, proj, gates, head_norm.reshape(1, c).astype(F32),
                skip.reshape(1, c).astype(F32), batch, seq)
    return _out_proj(y, w_out.astype(BF16), g_post, xf)


def _shared_kv(xf, batch, seq, kv_norm, kv_w, cmp_pos, cmp_w1, cmp_b1, cmp_w2, cmp_b2):
    ncmp = 2 * B_GROUPS * B_DK
    kv_w = kv_w.astype(BF16)
    kv_cmp = _rms_matmul(xf, kv_norm, kv_w[:, :ncmp], F32, split=True)
    kv_att = _rms_matmul(xf, kv_norm, kv_w[:, ncmp:], BF16, split=True)
    nr = seq // CMP_STRIDE
    half = CMP_STRIDE * B_DK
    pos_a = cmp_pos[:, :CMP_STRIDE].reshape(2, 1, half).astype(F32)
    pos_b = cmp_pos[:, CMP_STRIDE:].reshape(2, 1, half).astype(F32)
    kvc = _kv_compress(kv_cmp.reshape(2 * B_GROUPS, batch, nr, half), pos_a, pos_b,
                       cmp_w1[:, :half].astype(BF16), cmp_w1[:, half:].astype(BF16),
                       cmp_b1.reshape(2, 1, -1).astype(F32), cmp_w2.astype(BF16),
                       cmp_b2.reshape(2, 1, -1).astype(F32))
    return kvc, kv_att.reshape(4 * B_GROUPS, batch, seq, B_DK)


def _nsa_layer(xf, batch, seq, g_pre, g_post, w_in, w_out, shared, wov_t, n_sel):
    kvc, kv_att = shared
    nqk = B_HEADS * B_DK
    ngl = N_BRANCH * B_HEADS
    w_q = w_in[:, :nqk].astype(BF16)
    w_gl = w_in[:, nqk:nqk + ngl].reshape(-1, N_BRANCH, B_GROUPS, B_HPG)
    w_gl = jnp.transpose(w_gl, (0, 2, 1, 3)).reshape(-1, B_GROUPS, N_BRANCH * B_HPG)
    w_gl = jnp.pad(w_gl, ((0, 0), (0, 0), (0, LANES - N_BRANCH * B_HPG))).reshape(-1, B_GROUPS * LANES)
    w_r = jnp.concatenate([w_in[:, nqk + ngl:], w_gl], axis=1).astype(BF16)
    q = _rms_matmul(xf, g_pre, w_q, BF16, scale=LOG2E * float(B_DK) ** -0.5)
    proj_r = _rms_matmul(xf, g_pre, w_r, F32)
    oc, sel, blk_any = _nsa_cmp(q, kvc, proj_r, wov_t, batch, seq, n_sel)
    ns = seq // SEL_BLK
    bpc = SEL_CHUNK // SEL_BLK
    flags = blk_any[:, :, :, 0, :ns].reshape(batch, B_GROUPS, seq // Q_TILE, ns // bpc, bpc)
    flags = (jnp.max(flags, axis=-1) > 0).astype(jnp.int32)
    order = jnp.argsort(1 - flags, axis=-1, stable=True).astype(jnp.int32)
    count = jnp.sum(flags, axis=-1, keepdims=True)
    lists = jnp.concatenate([count, order, jnp.zeros_like(count)], axis=-1).reshape(-1)
    out = _nsa_sw(lists, q, sel, kv_att, proj_r, oc, batch, seq)
    return _out_proj(out, w_out.astype(BF16), g_post, xf)


def kernel(x, norm_pre, norm_post, a_w_in, a_conv_w, a_conv_b, a_w_q, a_w_k, a_w_v, a_w_gate,
           a_b_gate, a_head_norm, a_skip, a_w_out, kv_norm, kv_w, cmp_pos, cmp_w1, cmp_b1,
           cmp_w2, cmp_b2, b_w_in, b_w_out):
    batch, seq, d = x.shape
    na = a_w_in.shape[0]
    nb = b_w_in.shape[0]
    assert seq % (CMP_STRIDE * LANES) == 0 and seq % SEL_CHUNK == 0 and seq % A_CHUNK == 0
    xf = x.reshape(batch * seq, d).astype(F32)
    for l in range(na):
        xf = _mlstm_layer(xf, batch, seq, norm_pre[l], norm_post[l], a_w_in[l], a_conv_w[l],
                          a_conv_b[l], a_w_q[l], a_w_k[l], a_w_v[l], a_w_gate[l], a_b_gate[l],
                          a_head_norm[l], a_skip[l], a_w_out[l])
    shared = _shared_kv(xf, batch, seq, kv_norm, kv_w, cmp_pos, cmp_w1, cmp_b1, cmp_w2, cmp_b2)
    nr = seq // CMP_STRIDE
    ns = seq // SEL_BLK
    nsp = -(-ns // LANES) * LANES
    wov_t = _overlap_matrix_t(nr, nsp, nr - CMP_BLK // CMP_STRIDE + 1, ns)
    for l in range(nb):
        xf = _nsa_layer(xf, batch, seq, norm_pre[na + l], norm_post[na + l], b_w_in[l],
                        b_w_out[l], shared, wov_t, min(SEL_TOP, ns))
    return xf.reshape(batch, seq, d).astype(x.dtype)
```

```python
import functools

import jax
import jax.numpy as jnp
from jax import lax
from jax.experimental import pallas as pl
from jax.experimental.pallas import tpu as pltpu

F32 = jnp.float32
BF16 = jnp.bfloat16
EPS = 1e-6

A_HEADS = 4
A_CONV = 4
B_HEADS = 16
B_GROUPS = 2
B_HPG = B_HEADS // B_GROUPS
B_DK = 128
N_BRANCH = 3
CMP_BLK = 32
CMP_STRIDE = 16
CMP_HIDDEN = 256
SEL_BLK = 64
SEL_TOP = 16
WINDOW = 512

LANES = 128
SUBLANES = 8
MXU_DIM = 256
VMEM_LIMIT_BYTES = 56 * 1024 * 1024

A_CHUNK = 256
A_LOCAL_TILE = 256
Q_TILE = 256
SEL_CHUNK = 256
NEG = -1e30
M_INIT = -1e20
MASK_NEG = -2.0 ** 100
assert B_DK == LANES


def _params(n_axes):
    return pltpu.CompilerParams(
        dimension_semantics=("arbitrary",) * n_axes,
        vmem_limit_bytes=VMEM_LIMIT_BYTES)


def _sigmoid(v):
    return 1.0 / (1.0 + jnp.exp(-v))


def _silu(v):
    return v * _sigmoid(v)


def _shr(v, pow2):
    assert pow2 & (pow2 - 1) == 0
    return lax.shift_right_logical(v, jnp.int32(pow2.bit_length() - 1))


def _dot(a, b):
    return jnp.dot(a, b, preferred_element_type=F32)


def _dot_nt(a, b):
    return lax.dot_general(a, b, (((1,), (1,)), ((), ())), preferred_element_type=F32)


def _dot_tn(a, b):
    return lax.dot_general(a, b, (((0,), (0,)), ((), ())), preferred_element_type=F32)


def _store_proj(o_ref, acc, scale):
    if scale != 1.0:
        acc = acc * scale
    if len(o_ref.shape) == 2:
        o_ref[...] = acc.astype(o_ref.dtype)
    else:
        for j in range(o_ref.shape[0]):
            o_ref[j] = acc[:, j * LANES:(j + 1) * LANES].astype(o_ref.dtype)


def _rms_matmul_body(x_ref, g_ref, w_ref, o_ref, *h_ref, scale):
    x = x_ref[...]
    ms = jnp.mean(x * x, axis=-1, keepdims=True)
    h = ((x * lax.rsqrt(ms + EPS)) * g_ref[...]).astype(BF16)
    if h_ref:
        h_ref[0][...] = h
    _store_proj(o_ref, _dot(h, w_ref[...]), scale)


def _matmul_body(h_ref, w_ref, o_ref):
    _store_proj(o_ref, _dot(h_ref[...], w_ref[...]), 1.0)


def _pick_tile(n, candidates):
    for c in candidates:
        if n % c == 0:
            return c
    return n


def _proj_tiles(t, n, split, out_dtype):
    tm = _pick_tile(t, (1024, 512, 256, 128))
    tn = _pick_tile(n, (2048, 1536, 1280, 1024, 896, 768, 512, 256, 128))
    if split:
        out_spec = pl.BlockSpec((tn // LANES, tm, LANES), lambda j, i: (j, i, 0))
        out_shape = jax.ShapeDtypeStruct((n // LANES, t, LANES), out_dtype)
    else:
        out_spec = pl.BlockSpec((tm, tn), lambda j, i: (i, j))
        out_shape = jax.ShapeDtypeStruct((t, n), out_dtype)
    return tm, tn, out_spec, out_shape


def _rms_matmul(x, g, w, out_dtype, scale=1.0, split=False, emit_h=False):
    t, d = x.shape
    n = w.shape[1]
    tm, tn, out_spec, out_shape = _proj_tiles(t, n, split, out_dtype)
    out_specs, out_shapes = [out_spec], [out_shape]
    if emit_h:
        assert n == tn
        out_specs.append(pl.BlockSpec((tm, d), lambda j, i: (i, 0)))
        out_shapes.append(jax.ShapeDtypeStruct((t, d), BF16))
    res = pl.pallas_call(
        functools.partial(_rms_matmul_body, scale=scale),
        grid=(n // tn, t // tm),
        in_specs=[pl.BlockSpec((tm, d), lambda j, i: (i, 0)),
                  pl.BlockSpec((1, d), lambda j, i: (0, 0)),
                  pl.BlockSpec((d, tn), lambda j, i: (0, j))],
        out_specs=out_specs,
        out_shape=out_shapes,
        compiler_params=_params(2),
        name="rms_matmul",
    )(x, g.reshape(1, d).astype(F32), w)
    return res if emit_h else res[0]


def _matmul(h, w, out_dtype, split=False):
    t, d = h.shape
    n = w.shape[1]
    tm, tn, out_spec, out_shape = _proj_tiles(t, n, split, out_dtype)
    return pl.pallas_call(
        _matmul_body,
        grid=(n // tn, t // tm),
        in_specs=[pl.BlockSpec((tm, d), lambda j, i: (i, 0)),
                  pl.BlockSpec((d, tn), lambda j, i: (0, j))],
        out_specs=out_spec,
        out_shape=out_shape,
        compiler_params=_params(2),
        name="matmul",
    )(h, w)


def _out_proj_body(y_ref, w_ref, g_ref, x_ref, o_ref):
    r = _dot(y_ref[...], w_ref[...])
    ms = jnp.mean(r * r, axis=-1, keepdims=True)
    o_ref[...] = x_ref[...] + (r * lax.rsqrt(ms + EPS)) * g_ref[...]


def _out_proj(y, w, g, x):
    t, k = y.shape
    d = w.shape[1]
    tm = _pick_tile(t, (512, 256, 128))
    return pl.pallas_call(
        _out_proj_body,
        grid=(t // tm,),
        in_specs=[pl.BlockSpec((tm, k), lambda i: (i, 0)),
                  pl.BlockSpec((k, d), lambda i: (0, 0)),
                  pl.BlockSpec((1, d), lambda i: (0, 0)),
                  pl.BlockSpec((tm, d), lambda i: (i, 0))],
        out_specs=pl.BlockSpec((tm, d), lambda i: (i, 0)),
        out_shape=jax.ShapeDtypeStruct((t, d), F32),
        compiler_params=_params(1),
        name="out_proj",
    )(y, w, g.reshape(1, d).astype(F32), x)


HALO = SUBLANES


def _a_local_body(xm_ref, cw_ref, cb_ref, bdq_ref, bdk_ref, bdv_ref,
                  wgq_ref, wgk_ref, wgv_ref, bg_ref,
                  q_ref, k_ref, v_ref, xc_ref, gates_ref, ext_ref,
                  *, tl, tiles_per_seq, kscale):
    i = pl.program_id(0)
    c = xm_ref.shape[1]
    first = lax.rem(i, tiles_per_seq) == 0

    @pl.when(first)
    def _():
        ext_ref[0:HALO, :] = jnp.zeros((HALO, c), F32)

    @pl.when(jnp.logical_not(first))
    def _():
        ext_ref[0:HALO, :] = ext_ref[tl:tl + HALO, :]

    xm = xm_ref[...]
    ext_ref[HALO:HALO + tl, :] = xm
    acc = jnp.broadcast_to(cb_ref[...], (tl, c))
    for j in range(A_CONV):
        off = HALO - (A_CONV - 1) + j
        acc = acc + cw_ref[j:j + 1, :] * ext_ref[off:off + tl, :]
    xc = _silu(acc)
    xcb = xc.astype(BF16)
    xmb = xm.astype(BF16)
    xc_ref[...] = xcb
    gates = jnp.broadcast_to(bg_ref[...], (tl, LANES))
    for s in range(c // MXU_DIM):
        sl = slice(MXU_DIM * s, MXU_DIM * (s + 1))
        qb = _dot(xcb[:, sl], bdq_ref[s]).astype(BF16)
        kf = _dot(xcb[:, sl], bdk_ref[s])
        kb = kf.astype(BF16)
        vb = _dot(xmb[:, sl], bdv_ref[s]).astype(BF16)
        gates = gates + _dot(qb, wgq_ref[sl, :]) + _dot(kb, wgk_ref[sl, :]) + _dot(vb, wgv_ref[sl, :])
        q_ref[:, sl] = qb
        k_ref[:, sl] = (kf * kscale).astype(BF16)
        v_ref[:, sl] = vb
    gates_ref[...] = gates


def _a_local(xm_src, seq, cw, cb, bdq, bdk, bdv, wgq, wgk, wgv, bg, kscale):
    t = xm_src.shape[0]
    c = cw.shape[1]
    tl = A_LOCAL_TILE
    nslab = c // MXU_DIM
    full = lambda shape: pl.BlockSpec(shape, lambda i: (0,) * len(shape))
    tok = lambda w: pl.BlockSpec((tl, w), lambda i: (i, 0))
    return pl.pallas_call(
        functools.partial(_a_local_body, tl=tl, tiles_per_seq=seq // tl, kscale=kscale),
        grid=(t // tl,),
        in_specs=[tok(c), full((A_CONV, c)), full((1, c)),
                  full((nslab, MXU_DIM, MXU_DIM)), full((nslab, MXU_DIM, MXU_DIM)),
                  full((nslab, MXU_DIM, MXU_DIM)),
                  full((c, LANES)), full((c, LANES)), full((c, LANES)), full((1, LANES))],
        out_specs=[tok(c), tok(c), tok(c), tok(c), tok(LANES)],
        out_shape=[jax.ShapeDtypeStruct((t, c), BF16)] * 4 + [jax.ShapeDtypeStruct((t, LANES), F32)],
        scratch_shapes=[pltpu.VMEM((tl + 2 * HALO, c), F32)],
        compiler_params=_params(1),
        name="a_local",
    )(xm_src, cw, cb, bdq, bdk, bdv, wgq, wgk, wgv, bg)


def _a_cell_body(q_ref, k_ref, v_ref, xc_ref, o_ref, z_ref, gates_ref, hn_ref, skip_ref,
                 y_ref, c_ref, m_ref, *, chunk, dh):
    cidx = pl.program_id(1)
    nh = A_HEADS

    @pl.when(cidx == 0)
    def _():
        c_ref[...] = jnp.zeros(c_ref.shape, F32)
        m_ref[...] = jnp.zeros(m_ref.shape, F32)

    gates = gates_ref[...]
    lf = jnp.minimum(gates, 0.0) - jnp.log(1.0 + jnp.exp(-jnp.abs(gates)))
    r_i = lax.broadcasted_iota(jnp.int32, (chunk, chunk), 0)
    c_i = lax.broadcasted_iota(jnp.int32, (chunk, chunk), 1)
    causal = r_i >= c_i
    tril = causal.astype(BF16)
    hi = lf.astype(BF16)
    r1 = lf - hi.astype(F32)
    mid = r1.astype(BF16)
    lo = (r1 - mid.astype(F32)).astype(BF16)
    b_col = _dot(tril, hi) + _dot(tril, mid) + _dot(tril, lo)
    g_t = gates.T
    b_t = b_col.T

    for h in range(nh):
        hs = slice(h * dh, (h + 1) * dh)
        qh = q_ref[:, hs]
        kh = k_ref[:, hs]
        vh = v_ref[:, hs]
        i_row = g_t[h:h + 1, :]
        b_row = b_t[nh + h:nh + h + 1, :]
        i_c = gates[:, h:h + 1]
        b_c = b_col[:, nh + h:nh + h + 1]
        m_b = m_ref[h:h + 1, 0:1]
        log_d = jnp.where(causal, b_c - b_row + i_row, -jnp.inf)
        m_inter = b_c + m_b
        m_t = jnp.maximum(jnp.max(log_d, axis=1, keepdims=True), m_inter)
        d = jnp.exp(log_d - m_t)
        s = _dot_nt(qh, kh) * d
        decay = jnp.exp(m_inter - m_t)
        inter = _dot(qh, c_ref[h].astype(BF16))
        intra = _dot(s.astype(BF16), vh)
        num = decay * inter[:, :dh] + intra
        den = decay * inter[:, dh:] + jnp.sum(s, axis=1, keepdims=True)
        scale = 1.0 / jnp.maximum(jnp.abs(den), jnp.exp(-m_t))
        hcell = num * jnp.concatenate([scale] * (dh // LANES), axis=1)
        b_last = b_col[chunk - 1:chunk, nh + h:nh + h + 1]
        m_new = jnp.maximum(b_last + m_b,
                            jnp.max(b_last - b_row + i_row, axis=1, keepdims=True))
        w_c = jnp.exp(b_last - b_c + i_c - m_new)
        carry = jnp.exp(b_last + m_b - m_new)
        wv = jnp.concatenate([vh.astype(F32) * w_c, jnp.broadcast_to(w_c, (chunk, LANES))],
                             axis=1).astype(BF16)
        c_ref[h] = carry * c_ref[h] + _dot_tn(kh, wv)
        m_ref[h:h + 1, :] = jnp.broadcast_to(m_new, (1, LANES))
        hc = _sigmoid(o_ref[:, hs]) * hcell
        mu = jnp.mean(hc, axis=1, keepdims=True)
        cen = hc - mu
        var = jnp.mean(cen * cen, axis=1, keepdims=True)
        hn = cen * lax.rsqrt(var + EPS) * hn_ref[:, hs]
        y = (hn + skip_ref[:, hs] * xc_ref[:, hs].astype(F32)) * _silu(z_ref[:, hs])
        y_ref[:, hs] = y.astype(BF16)


def _a_cell(q, k, v, xc, proj, gates, head_norm, skip, batch, seq):
    t, c = q.shape
    chunk = A_CHUNK
    nc = seq // chunk
    dh = c // A_HEADS
    tokc = lambda j: pl.BlockSpec((chunk, c), lambda b, i, j=j: (b * nc + i, j))
    full = lambda shape: pl.BlockSpec(shape, lambda b, i: (0,) * len(shape))
    return pl.pallas_call(
        functools.partial(_a_cell_body, chunk=chunk, dh=dh),
        grid=(batch, nc),
        in_specs=[tokc(0), tokc(0), tokc(0), tokc(0), tokc(1), tokc(2),
                  pl.BlockSpec((chunk, LANES), lambda b, i: (b * nc + i, 0)),
                  full((1, c)), full((1, c))],
        out_specs=tokc(0),
        out_shape=jax.ShapeDtypeStruct((t, c), BF16),
        scratch_shapes=[pltpu.VMEM((A_HEADS, dh, dh + LANES), F32),
                        pltpu.VMEM((SUBLANES, LANES), F32)],
        compiler_params=_params(2),
        name="a_cell",
    )(q, k, v, xc, proj, proj, gates, head_norm, skip)


def _kv_compress_body(x_ref, pa_ref, pb_ref, w1a_ref, w1b_ref, b1_ref, w2_ref, b2_ref,
                      o_ref, nxt_ref):
    x = x_ref[0, 0]
    nr = x.shape[0]
    first = _dot((x + pa_ref[0]).astype(BF16), w1a_ref[0])
    nxt_ref[0:nr, :] = _dot((x + pb_ref[0]).astype(BF16), w1b_ref[0])
    nxt_ref[nr:nr + SUBLANES, :] = jnp.zeros((SUBLANES, nxt_ref.shape[1]), F32)
    h1 = _silu(first + nxt_ref[1:nr + 1, :] + b1_ref[0])
    o_ref[0, 0] = (_dot(h1.astype(BF16), w2_ref[0]) + b2_ref[0]).astype(o_ref.dtype)


def _kv_compress(x16, pos_a, pos_b, w1a, w1b, b1, w2, b2):
    nkg, batch, nr, half = x16.shape
    hid = w1a.shape[2]
    dk = w2.shape[2]
    kind = lambda shape: pl.BlockSpec((1,) + shape, lambda p, b: (p // B_GROUPS, 0, 0))
    return pl.pallas_call(
        _kv_compress_body,
        grid=(nkg, batch),
        in_specs=[pl.BlockSpec((1, 1, nr, half), lambda p, b: (p, b, 0, 0)),
                  kind((1, half)), kind((1, half)),
                  kind((half, hid)), kind((half, hid)), kind((1, hid)),
                  kind((hid, dk)), kind((1, dk))],
        out_specs=pl.BlockSpec((1, 1, nr, dk), lambda p, b: (p, b, 0, 0)),
        out_shape=jax.ShapeDtypeStruct((nkg, batch, nr, dk), BF16),
        scratch_shapes=[pltpu.VMEM((nr + SUBLANES, hid), F32)],
        compiler_params=_params(2),
        name="kv_compress",
    )(x16, pos_a, pos_b, w1a, w1b, b1, w2, b2)


def _stack_heads(q):
    return jnp.concatenate([q[:, h * B_DK:(h + 1) * B_DK] for h in range(B_HPG)], axis=0)


LOG2E = 1.4426950408889634


def _head_slopes(g):
    return [LOG2E * jnp.exp2(jnp.full((1, LANES), -0.5, F32) * (g * B_HPG + h + 1).astype(F32))
            for h in range(B_HPG)]


def _lane_tiles(a):
    return [a[:, j * LANES:(j + 1) * LANES] for j in range(a.shape[1] // LANES)]


def _head_logits(s, h, tq, slope, rel, madd):
    rows = slice(h * tq, (h + 1) * tq)
    bias = [slope * r for r in _lane_tiles(rel)]
    if madd is not None:
        bias = [b_ + ma for b_, ma in zip(bias, _lane_tiles(madd))]
    return [s[rows, j * LANES:(j + 1) * LANES] + b_ for j, b_ in enumerate(bias)]


def _with_ones(v):
    return jnp.concatenate([v, jnp.ones_like(v)], axis=1)


def _tile_max(tiles):
    mx = tiles[0]
    for t in tiles[1:]:
        mx = jnp.maximum(mx, t)
    return jnp.max(mx, axis=1, keepdims=True)


def _tile_sum(tiles):
    sm = tiles[0]
    for t in tiles[1:]:
        sm = sm + t
    return jnp.sum(sm, axis=1, keepdims=True)


def _nsa_cmp_body(q_ref, kc_ref, vc_ref, zg_ref, gl_ref, wov_ref,
                  oc_ref, sel_ref, any_ref, o_scr, imp_scr, *, tq, n_sel):
    g = pl.program_id(1)
    t0 = pl.program_id(2) * tq
    ncp = kc_ref.shape[2]
    nsp = wov_ref.shape[0]

    def attend(width):
        qs = _stack_heads(q_ref[...])
        slopes = _head_slopes(g)
        n = lax.broadcasted_iota(jnp.int32, (1, width), 1)
        rel = ((n * CMP_STRIDE - t0).astype(F32) + 0.5 * (CMP_BLK - 1))
        c_end = lax.broadcasted_iota(jnp.int32, (tq, width), 1) * CMP_STRIDE + (CMP_BLK - 1)
        tok = t0 + lax.broadcasted_iota(jnp.int32, (tq, width), 0)
        madd = jnp.where(c_end <= tok, 0.0, NEG)
        s = _dot_nt(qs, kc_ref[0, 0, 0:width, :])
        p_rows = []
        psum_tiles = None
        for h in range(B_HPG):
            tiles = _head_logits(s, h, tq, slopes[h], rel, madd)
            m = jnp.maximum(_tile_max(tiles), M_INIT)
            es = [jnp.exp2(t - m) for t in tiles]
            inv = 1.0 / jnp.maximum(_tile_sum(es), 1e-30)
            ps = [e * inv for e in es]
            psum_tiles = ps if psum_tiles is None else [a + b_ for a, b_ in zip(psum_tiles, ps)]
            p_rows.append(jnp.concatenate([p.astype(BF16) for p in ps], axis=1))
        o_scr[...] = _dot(jnp.concatenate(p_rows, axis=0), vc_ref[0, 0, 0:width, :])
        psum = jnp.concatenate(psum_tiles, axis=1)
        p_hi = psum.astype(BF16)
        p_lo = (psum - p_hi.astype(F32)).astype(BF16)
        wov = wov_ref[:, 0:width]
        imp_scr[...] = _dot_nt(wov, p_hi) + _dot_nt(wov, p_lo)

    ntile = ncp // LANES
    need = _shr(t0 + tq - 1, CMP_STRIDE * LANES) + 1
    for k in range(1, ntile + 1):
        @pl.when((need == k) if k < ntile else (need >= k))
        def _(k=k):
            attend(k * LANES)

    o = o_scr[...]
    imp = imp_scr[...]
    j = lax.broadcasted_iota(jnp.int32, (nsp, tq), 0)
    cur = _shr(t0 + lax.broadcasted_iota(jnp.int32, (1, tq), 1), SEL_BLK)
    forced = (j == 0) | (j == cur) | (j == cur - 1)
    val = jnp.where(forced, jnp.inf, jnp.where(j <= cur, imp, -jnp.inf))
    sel_t = jnp.zeros((nsp, tq), F32)
    for _ in range(n_sel):
        mx = jnp.max(val, axis=0, keepdims=True)
        cand = jnp.where((val == mx) & (mx > -jnp.inf), j, nsp)
        pick = j == jnp.min(cand, axis=0, keepdims=True)
        sel_t = jnp.where(pick, 1.0, sel_t)
        val = jnp.where(pick, -jnp.inf, val)
    sel = sel_t.T
    sel_ref[0, 0] = sel.astype(sel_ref.dtype)
    any_ref[0, 0, 0] = jnp.broadcast_to(jnp.max(sel, axis=0, keepdims=True), (SUBLANES, nsp))
    gate = _sigmoid(gl_ref[...])
    for h in range(B_HPG):
        hs = slice(h * B_DK, (h + 1) * B_DK)
        oc_ref[:, hs] = gate[:, h:h + 1] * _silu(zg_ref[:, hs]) * o[h * tq:(h + 1) * tq]


def _nsa_cmp(q, kvc, proj_r, wov_t, batch, seq, n_sel):
    t = q.shape[0]
    tq = Q_TILE
    nqb = seq // tq
    ncp = kvc.shape[2]
    nsp = wov_t.shape[0]
    gw = B_HPG * B_DK
    zoff = 0
    goff = (N_BRANCH * B_HEADS * B_DK) // LANES
    return pl.pallas_call(
        functools.partial(_nsa_cmp_body, tq=tq, n_sel=n_sel),
        grid=(batch, B_GROUPS, nqb),
        in_specs=[pl.BlockSpec((tq, gw), lambda b, g, i: (b * nqb + i, g)),
                  pl.BlockSpec((1, 1, ncp, B_DK), lambda b, g, i: (g, b, 0, 0)),
                  pl.BlockSpec((1, 1, ncp, B_DK), lambda b, g, i: (B_GROUPS + g, b, 0, 0)),
                  pl.BlockSpec((tq, gw), lambda b, g, i: (b * nqb + i, zoff + g)),
                  pl.BlockSpec((tq, LANES), lambda b, g, i: (b * nqb + i, goff + g)),
                  pl.BlockSpec((nsp, ncp), lambda b, g, i: (0, 0))],
        out_specs=[pl.BlockSpec((tq, gw), lambda b, g, i: (b * nqb + i, g)),
                   pl.BlockSpec((1, 1, tq, nsp), lambda b, g, i: (b, g, i, 0)),
                   pl.BlockSpec((1, 1, 1, SUBLANES, nsp), lambda b, g, i: (b, g, i, 0, 0))],
        out_shape=[jax.ShapeDtypeStruct((t, B_GROUPS * gw), F32),
                   jax.ShapeDtypeStruct((batch, B_GROUPS, seq, nsp), BF16),
                   jax.ShapeDtypeStruct((batch, B_GROUPS, nqb, SUBLANES, nsp), F32)],
        scratch_shapes=[pltpu.VMEM((B_HPG * tq, B_DK), F32), pltpu.VMEM((nsp, tq), F32)],
        compiler_params=_params(3),
        name="nsa_cmp",
    )(q, kvc, kvc, proj_r, proj_r, wov_t)


def _softmax_once(s, v, slopes, rel, madd, tq):
    p_rows = []
    for h in range(B_HPG):
        tiles = _head_logits(s, h, tq, slopes[h], rel, madd)
        m = jnp.maximum(_tile_max(tiles), M_INIT)
        p_rows.append(jnp.concatenate([jnp.exp2(t - m).astype(BF16) for t in tiles], axis=1))
    pv = _dot(jnp.concatenate(p_rows, axis=0), _with_ones(v))
    return pv[:, :B_DK] * (1.0 / jnp.maximum(pv[:, B_DK:], 1e-30))


def _online_update(s, v_c, slopes, rel, madd, m_ref, l_ref, acc_ref, tq):
    p_rows = []
    alphas = []
    m_news = []
    for h in range(B_HPG):
        rows = slice(h * tq, (h + 1) * tq)
        m_prev = m_ref[rows, :]
        m_new = jnp.maximum(m_prev, _tile_max(_head_logits(s, h, tq, slopes[h], rel, madd)))
        m_ref[rows, :] = m_new
        alphas.append(jnp.exp2(m_prev - m_new))
        m_news.append(m_new)
    for h in range(B_HPG):
        tiles = _head_logits(s, h, tq, slopes[h], rel, madd)
        p_rows.append(jnp.concatenate([jnp.exp2(t - m_news[h]).astype(BF16) for t in tiles], axis=1))
    pv = _dot(jnp.concatenate(p_rows, axis=0), _with_ones(v_c))
    for h in range(B_HPG):
        rows = slice(h * tq, (h + 1) * tq)
        acc_ref[rows, :] = alphas[h] * acc_ref[rows, :] + pv[rows, :B_DK]
        l_ref[rows, :] = alphas[h] * l_ref[rows, :] + pv[rows, B_DK:]


def _softmax_reset(m_ref, l_ref, acc_ref):
    m_ref[...] = jnp.full(m_ref.shape, M_INIT, F32)
    l_ref[...] = jnp.zeros(l_ref.shape, F32)
    acc_ref[...] = jnp.zeros(acc_ref.shape, F32)


def _softmax_result(l_ref, acc_ref):
    return acc_ref[...] * (1.0 / jnp.maximum(l_ref[...], 1e-30))


def _nsa_sw_body(lst_ref, q_ref, sel_ref, ks_ref, vs_ref, kw_ref, vw_ref,
                 zs_ref, zw_ref, gl_ref, oc_ref, out_ref,
                 qs_ref, s_a, s_b, m_ref, l_ref, acc_ref,
                 *, tq, nqb, nch):
    b = pl.program_id(0)
    g = pl.program_id(1)
    i = pl.program_id(2)
    t0 = i * tq
    qs_ref[:, :B_DK] = _stack_heads(q_ref[...])
    not_sel = (1.0 - sel_ref[0, 0].astype(F32)).astype(BF16)
    qs_ref[:, B_DK:] = jnp.concatenate([not_sel] * B_HPG, axis=0)
    slopes = _head_slopes(g)
    nsp = sel_ref.shape[3]
    ch = SEL_CHUNK
    bpc = ch // SEL_BLK
    lbase = ((b * B_GROUPS + g) * nqb + i) * (nch + 2)
    n_reg = lst_ref[lbase] - 1

    def positions(start, width):
        rel = (start - t0) + lax.broadcasted_iota(jnp.int32, (1, width), 1)
        dist = lax.broadcasted_iota(jnp.int32, (tq, width), 0) - (
            (start - t0) + lax.broadcasted_iota(jnp.int32, (tq, width), 1))
        return rel.astype(F32), dist

    def scores(c):
        start = pl.multiple_of(c * ch, ch)
        key_blk = c * bpc + _shr(lax.broadcasted_iota(jnp.int32, (ch, nsp), 0), SEL_BLK)
        own_blk = jnp.where(lax.broadcasted_iota(jnp.int32, (ch, nsp), 1) == key_blk, MASK_NEG, 0.0)
        keys = jnp.concatenate([ks_ref[0, 0, pl.ds(start, ch), :], own_blk.astype(BF16)], axis=1)
        return _dot_nt(qs_ref[...], keys)

    def stage(j, s_ref):
        s_ref[...] = scores(lst_ref[lbase + 1 + j])

    def consume(j, s_ref):
        start = pl.multiple_of(lst_ref[lbase + 1 + j] * ch, ch)
        rel, _ = positions(start, ch)
        _online_update(s_ref, vs_ref[0, 0, pl.ds(start, ch), :], slopes, rel, None,
                       m_ref, l_ref, acc_ref, tq)

    def consume_own(s_ref):
        start = pl.multiple_of(lst_ref[lbase + 1 + n_reg] * ch, ch)
        rel, dist = positions(start, ch)
        _online_update(s_ref, vs_ref[0, 0, pl.ds(start, ch), :], slopes, rel,
                       jnp.where(dist >= 0, 0.0, NEG), m_ref, l_ref, acc_ref, tq)

    _softmax_reset(m_ref, l_ref, acc_ref)
    stage(0, s_a)

    def chunk_pair(jj, carry):
        j = 2 * jj
        stage(j + 1, s_b)
        consume(j, s_a)

        @pl.when(j + 1 < n_reg)
        def _():
            stage(j + 2, s_a)
            consume(j + 1, s_b)
        return carry

    lax.fori_loop(0, _shr(n_reg + 1, 2), chunk_pair, 0)

    @pl.when((n_reg & 1) == 0)
    def _():
        consume_own(s_a)

    @pl.when((n_reg & 1) == 1)
    def _():
        consume_own(s_b)

    o_s = _softmax_result(l_ref, acc_ref)

    nwin = WINDOW // ch + max(tq // ch, 1)
    c_lo = jnp.maximum(_shr(t0 + tq - 1, ch) - (nwin - 1), 0)
    st = pl.multiple_of(c_lo * ch, ch)
    rel, dist = positions(st, nwin * ch)
    madd = jnp.where(dist >= 0, jnp.where(dist < WINDOW, 0.0, NEG), NEG)
    o_w = _softmax_once(_dot_nt(qs_ref[:, :B_DK], kw_ref[0, 0, pl.ds(st, nwin * ch), :]),
                        vw_ref[0, 0, pl.ds(st, nwin * ch), :], slopes, rel, madd, tq)

    gate = _sigmoid(gl_ref[...])
    for h in range(B_HPG):
        hs = slice(h * B_DK, (h + 1) * B_DK)
        rs = slice(h * tq, (h + 1) * tq)
        merged = (oc_ref[:, hs]
                  + gate[:, B_HPG + h:B_HPG + h + 1] * _silu(zs_ref[:, hs]) * o_s[rs]
                  + gate[:, 2 * B_HPG + h:2 * B_HPG + h + 1] * _silu(zw_ref[:, hs]) * o_w[rs])
        out_ref[:, hs] = merged.astype(out_ref.dtype)


def _nsa_sw(flags, q, sel, kv_att, proj_r, oc, batch, seq):
    t = q.shape[0]
    tq = Q_TILE
    nqb = seq // tq
    nch = seq // SEL_CHUNK
    nsp = sel.shape[3]
    gw = B_HPG * B_DK
    goff = (N_BRANCH * B_HEADS * B_DK) // LANES
    rows = B_HPG * tq
    tok = lambda j: pl.BlockSpec((tq, gw), lambda b, g, i, f, j=j: (b * nqb + i, j * B_GROUPS + g))
    res = lambda k: pl.BlockSpec((1, 1, seq, B_DK), lambda b, g, i, f, k=k: (k * B_GROUPS + g, b, 0, 0))
    grid_spec = pltpu.PrefetchScalarGridSpec(
        num_scalar_prefetch=1,
        grid=(batch, B_GROUPS, nqb),
        in_specs=[tok(0),
                  pl.BlockSpec((1, 1, tq, nsp), lambda b, g, i, f: (b, g, i, 0)),
                  res(0), res(1), res(2), res(3),
                  tok(1), tok(2),
                  pl.BlockSpec((tq, LANES), lambda b, g, i, f: (b * nqb + i, goff + g)),
                  tok(0)],
        out_specs=tok(0),
        scratch_shapes=[pltpu.VMEM((rows, B_DK + nsp), BF16),
                        pltpu.VMEM((rows, SEL_CHUNK), F32),
                        pltpu.VMEM((rows, SEL_CHUNK), F32),
                        pltpu.VMEM((rows, LANES), F32),
                        pltpu.VMEM((rows, LANES), F32),
                        pltpu.VMEM((rows, B_DK), F32)])
    return pl.pallas_call(
        functools.partial(_nsa_sw_body, tq=tq, nqb=nqb, nch=nch),
        grid_spec=grid_spec,
        out_shape=jax.ShapeDtypeStruct((t, B_GROUPS * gw), BF16),
        compiler_params=_params(3),
        name="nsa_sw",
    )(flags, q, sel, kv_att, kv_att, kv_att, kv_att, proj_r, proj_r, proj_r, oc)


def _blockdiag_dense(w):
    nb, bs, _ = w.shape
    per = MXU_DIM // bs
    w4 = w.reshape(nb // per, per, bs, bs)
    dense = jnp.einsum("sgij,gh->sgihj", w4, jnp.eye(per, dtype=w.dtype))
    return dense.reshape(nb // per, MXU_DIM, MXU_DIM).astype(BF16)


def _pad_cols(w, width):
    return jnp.pad(w, ((0, 0), (0, width - w.shape[1])))


def _overlap_matrix_t(nr, nsp, nc, ns):
    i = jnp.arange(nr)[None, :] * CMP_STRIDE
    j = jnp.arange(nsp)[:, None] * SEL_BLK
    ov = jnp.minimum(i + CMP_BLK, j + SEL_BLK) - jnp.maximum(i, j)
    ov = jnp.maximum(ov, 0) / CMP_STRIDE
    live = (jnp.arange(nr)[None, :] < nc) & (jnp.arange(nsp)[:, None] < ns)
    return jnp.where(live, ov, 0).astype(BF16)


def _mlstm_layer(xf, batch, seq, g_pre, g_post, w_in, conv_w, conv_b, w_q, w_k, w_v,
                 w_gate, b_gate, head_norm, skip, w_out):
    c = conv_w.shape[1]
    dh = c // A_HEADS
    proj = _rms_matmul(xf, g_pre, w_in.astype(BF16), F32)
    wg = [_pad_cols(w_gate[j * c:(j + 1) * c], LANES).astype(BF16) for j in range(3)]
    bg = _pad_cols(b_gate.reshape(1, -1), LANES).astype(F32)
    q, k, v, xc, gates = _a_local(
        proj, seq, conv_w.astype(F32), conv_b.reshape(1, c).astype(F32),
        _blockdiag_dense(w_q), _blockdiag_dense(w_k), _blockdiag_dense(w_v),
        wg[0], wg[1], wg[2], bg, float(dh) ** -0.5)
    y = _a_cell(q, k, v, xc, proj, gates, head_norm.reshape(1, c).astype(F32),
                skip.reshape(1, c).astype(F32), batch, seq)
    return _out_proj(y, w_out.astype(BF16), g_post, xf)


def _shared_kv(xf, batch, seq, kv_norm, kv_w, cmp_pos, cmp_w1, cmp_b1, cmp_w2, cmp_b2):
    ncmp = 2 * B_GROUPS * B_DK
    kv_w = kv_w.astype(BF16)
    kv_cmp, h = _rms_matmul(xf, kv_norm, kv_w[:, :ncmp], F32, split=True, emit_h=True)
    kv_att = _matmul(h, kv_w[:, ncmp:], BF16, split=True)
    nr = seq // CMP_STRIDE
    half = CMP_STRIDE * B_DK
    pos_a = cmp_pos[:, :CMP_STRIDE].reshape(2, 1, half).astype(F32)
    pos_b = cmp_pos[:, CMP_STRIDE:].reshape(2, 1, half).astype(F32)
    kvc = _kv_compress(kv_cmp.reshape(2 * B_GROUPS, batch, nr, half), pos_a, pos_b,
                       cmp_w1[:, :half].astype(BF16), cmp_w1[:, half:].astype(BF16),
                       cmp_b1.reshape(2, 1, -1).astype(F32), cmp_w2.astype(BF16),
                       cmp_b2.reshape(2, 1, -1).astype(F32))
    return kvc, kv_att.reshape(4 * B_GROUPS, batch, seq, B_DK)


def _nsa_layer(xf, batch, seq, g_pre, g_post, w_in, w_out, shared, wov_t, n_sel):
    kvc, kv_att = shared
    nqk = B_HEADS * B_DK
    ngl = N_BRANCH * B_HEADS
    w_q = w_in[:, :nqk].astype(BF16)
    w_gl = w_in[:, nqk:nqk + ngl].reshape(-1, N_BRANCH, B_GROUPS, B_HPG)
    w_gl = jnp.transpose(w_gl, (0, 2, 1, 3)).reshape(-1, B_GROUPS, N_BRANCH * B_HPG)
    w_gl = jnp.pad(w_gl, ((0, 0), (0, 0), (0, LANES - N_BRANCH * B_HPG))).reshape(-1, B_GROUPS * LANES)
    w_r = jnp.concatenate([w_in[:, nqk + ngl:], w_gl], axis=1).astype(BF16)
    q, h = _rms_matmul(xf, g_pre, w_q, BF16, scale=LOG2E * float(B_DK) ** -0.5, emit_h=True)
    proj_r = _matmul(h, w_r, F32)
    oc, sel, blk_any = _nsa_cmp(q, kvc, proj_r, wov_t, batch, seq, n_sel)
    ns = seq // SEL_BLK
    bpc = SEL_CHUNK // SEL_BLK
    flags = blk_any[:, :, :, 0, :ns].reshape(batch, B_GROUPS, seq // Q_TILE, ns // bpc, bpc)
    flags = (jnp.max(flags, axis=-1) > 0).astype(jnp.int32)
    order = jnp.argsort(1 - flags, axis=-1, stable=True).astype(jnp.int32)
    count = jnp.sum(flags, axis=-1, keepdims=True)
    lists = jnp.concatenate([count, order, jnp.zeros_like(count)], axis=-1).reshape(-1)
    out = _nsa_sw(lists, q, sel, kv_att, proj_r, oc, batch, seq)
    return _out_proj(out, w_out.astype(BF16), g_post, xf)


def kernel(x, norm_pre, norm_post, a_w_in, a_conv_w, a_conv_b, a_w_q, a_w_k, a_w_v, a_w_gate,
           a_b_gate, a_head_norm, a_skip, a_w_out, kv_norm, kv_w, cmp_pos, cmp_w1, cmp_b1,
           cmp_w2, cmp_b2, b_w_in, b_w_out):
    batch, seq, d = x.shape
    na = a_w_in.shape[0]
    nb = b_w_in.shape[0]
    assert seq % (CMP_STRIDE * LANES) == 0 and seq % SEL_CHUNK == 0 and seq % A_CHUNK == 0
    xf = x.reshape(batch * seq, d).astype(F32)
    for l in range(na):
        xf = _mlstm_layer(xf, batch, seq, norm_pre[l], norm_post[l], a_w_in[l], a_conv_w[l],
                          a_conv_b[l], a_w_q[l], a_w_k[l], a_w_v[l], a_w_gate[l], a_b_gate[l],
                          a_head_norm[l], a_skip[l], a_w_out[l])
    shared = _shared_kv(xf, batch, seq, kv_norm, kv_w, cmp_pos, cmp_w1, cmp_b1, cmp_w2, cmp_b2)
    nr = seq // CMP_STRIDE
    ns = seq // SEL_BLK
    nsp = -(-ns // LANES) * LANES
    wov_t = _overlap_matrix_t(nr, nsp, nr - CMP_BLK // CMP_STRIDE + 1, ns)
    for l in range(nb):
        xf = _nsa_layer(xf, batch, seq, norm_pre[na + l], norm_post[na + l], b_w_in[l],
                        b_w_out[l], shared, wov_t, min(SEL_TOP, ns))
    return xf.reshape(batch, seq, d).astype(x.dtype)
```

```python
import functools

import jax
import jax.numpy as jnp
from jax import lax
from jax.experimental import pallas as pl
from jax.experimental.pallas import tpu as pltpu

F32 = jnp.float32
BF16 = jnp.bfloat16
EPS = 1e-6

A_HEADS = 4
A_CONV = 4
B_HEADS = 16
B_GROUPS = 2
B_HPG = B_HEADS // B_GROUPS
B_DK = 128
N_BRANCH = 3
CMP_BLK = 32
CMP_STRIDE = 16
CMP_HIDDEN = 256
SEL_BLK = 64
SEL_TOP = 16
WINDOW = 512

LANES = 128
SUBLANES = 8
MXU_DIM = 256
VMEM_LIMIT_BYTES = 56 * 1024 * 1024

A_CHUNK = 256
A_LOCAL_TILE = 256
Q_TILE = 256
SEL_CHUNK = 256
NEG = -1e30
M_INIT = -1e20
MASK_NEG = -2.0 ** 100
assert B_DK == LANES


def _params(n_axes):
    return pltpu.CompilerParams(
        dimension_semantics=("arbitrary",) * n_axes,
        vmem_limit_bytes=VMEM_LIMIT_BYTES)


def _sigmoid(v):
    return 1.0 / (1.0 + jnp.exp(-v))


def _silu(v):
    return v * _sigmoid(v)


def _shr(v, pow2):
    assert pow2 & (pow2 - 1) == 0
    return lax.shift_right_logical(v, jnp.int32(pow2.bit_length() - 1))


def _dot(a, b):
    return jnp.dot(a, b, preferred_element_type=F32)


def _dot_nt(a, b):
    return lax.dot_general(a, b, (((1,), (1,)), ((), ())), preferred_element_type=F32)


def _dot_tn(a, b):
    return lax.dot_general(a, b, (((0,), (0,)), ((), ())), preferred_element_type=F32)


_ACTS = {None: lambda v: v, "sigmoid": _sigmoid, "silu": _silu}


def _store_proj(o_ref, acc, scale, acts):
    if scale != 1.0:
        acc = acc * scale

    def store(act):
        val = _ACTS[act](acc)
        if len(o_ref.shape) == 2:
            o_ref[...] = val.astype(o_ref.dtype)
        else:
            for j in range(o_ref.shape[0]):
                o_ref[j] = val[:, j * LANES:(j + 1) * LANES].astype(o_ref.dtype)

    if len(acts) == 1:
        store(acts[0])
    else:
        for idx, act in enumerate(acts):
            pl.when(pl.program_id(0) == idx)(functools.partial(store, act))


def _rms_matmul_body(x_ref, g_ref, w_ref, o_ref, *h_ref, scale, acts):
    x = x_ref[...]
    ms = jnp.mean(x * x, axis=-1, keepdims=True)
    h = ((x * lax.rsqrt(ms + EPS)) * g_ref[...]).astype(BF16)
    if h_ref:
        h_ref[0][...] = h
    _store_proj(o_ref, _dot(h, w_ref[...]), scale, acts)


def _matmul_body(h_ref, w_ref, o_ref, *, acts):
    _store_proj(o_ref, _dot(h_ref[...], w_ref[...]), 1.0, acts)


def _pick_tile(n, candidates):
    for c in candidates:
        if n % c == 0:
            return c
    return n


def _proj_tiles(t, n, split, out_dtype):
    tm = _pick_tile(t, (1024, 512, 256, 128))
    tn = _pick_tile(n, (2048, 1536, 1280, 1024, 896, 768, 512, 256, 128))
    if split:
        out_spec = pl.BlockSpec((tn // LANES, tm, LANES), lambda j, i: (j, i, 0))
        out_shape = jax.ShapeDtypeStruct((n // LANES, t, LANES), out_dtype)
    else:
        out_spec = pl.BlockSpec((tm, tn), lambda j, i: (i, j))
        out_shape = jax.ShapeDtypeStruct((t, n), out_dtype)
    return tm, tn, out_spec, out_shape


def _rms_matmul(x, g, w, out_dtype, scale=1.0, split=False, emit_h=False, acts=(None,)):
    t, d = x.shape
    n = w.shape[1]
    tm, tn, out_spec, out_shape = _proj_tiles(t, n, split, out_dtype)
    assert len(acts) in (1, n // tn)
    out_specs, out_shapes = [out_spec], [out_shape]
    if emit_h:
        assert n == tn
        out_specs.append(pl.BlockSpec((tm, d), lambda j, i: (i, 0)))
        out_shapes.append(jax.ShapeDtypeStruct((t, d), BF16))
    res = pl.pallas_call(
        functools.partial(_rms_matmul_body, scale=scale, acts=acts),
        grid=(n // tn, t // tm),
        in_specs=[pl.BlockSpec((tm, d), lambda j, i: (i, 0)),
                  pl.BlockSpec((1, d), lambda j, i: (0, 0)),
                  pl.BlockSpec((d, tn), lambda j, i: (0, j))],
        out_specs=out_specs,
        out_shape=out_shapes,
        compiler_params=_params(2),
        name="rms_matmul",
    )(x, g.reshape(1, d).astype(F32), w)
    return res if emit_h else res[0]


def _matmul(h, w, out_dtype, split=False, act=None):
    t, d = h.shape
    n = w.shape[1]
    tm, tn, out_spec, out_shape = _proj_tiles(t, n, split, out_dtype)
    return pl.pallas_call(
        functools.partial(_matmul_body, acts=(act,)),
        grid=(n // tn, t // tm),
        in_specs=[pl.BlockSpec((tm, d), lambda j, i: (i, 0)),
                  pl.BlockSpec((d, tn), lambda j, i: (0, j))],
        out_specs=out_spec,
        out_shape=out_shape,
        compiler_params=_params(2),
        name="matmul",
    )(h, w)


def _out_proj_body(y_ref, w_ref, g_ref, x_ref, o_ref):
    r = _dot(y_ref[...], w_ref[...])
    ms = jnp.mean(r * r, axis=-1, keepdims=True)
    o_ref[...] = x_ref[...] + (r * lax.rsqrt(ms + EPS)) * g_ref[...]


def _out_proj(y, w, g, x):
    t, k = y.shape
    d = w.shape[1]
    tm = _pick_tile(t, (512, 256, 128))
    return pl.pallas_call(
        _out_proj_body,
        grid=(t // tm,),
        in_specs=[pl.BlockSpec((tm, k), lambda i: (i, 0)),
                  pl.BlockSpec((k, d), lambda i: (0, 0)),
                  pl.BlockSpec((1, d), lambda i: (0, 0)),
                  pl.BlockSpec((tm, d), lambda i: (i, 0))],
        out_specs=pl.BlockSpec((tm, d), lambda i: (i, 0)),
        out_shape=jax.ShapeDtypeStruct((t, d), F32),
        compiler_params=_params(1),
        name="out_proj",
    )(y, w, g.reshape(1, d).astype(F32), x)


HALO = SUBLANES


def _a_local_body(xm_ref, cw_ref, cb_ref, bdq_ref, bdk_ref, bdv_ref,
                  wgq_ref, wgk_ref, wgv_ref, bg_ref,
                  q_ref, k_ref, v_ref, xc_ref, gates_ref, ext_ref, cv_ref,
                  *, tl, tiles_per_seq, kscale):
    i = pl.program_id(0)
    c = xm_ref.shape[1]
    first = lax.rem(i, tiles_per_seq) == 0

    @pl.when(first)
    def _():
        ext_ref[0:HALO, :] = jnp.zeros((HALO, c), F32)

    @pl.when(jnp.logical_not(first))
    def _():
        ext_ref[0:HALO, :] = ext_ref[tl:tl + HALO, :]

    xm = xm_ref[...]
    ext_ref[HALO:HALO + tl, :] = xm
    acc = jnp.broadcast_to(cb_ref[...], (tl, c))
    for j in range(A_CONV):
        off = HALO - (A_CONV - 1) + j
        acc = acc + cw_ref[j:j + 1, :] * ext_ref[off:off + tl, :]
    cv_ref[...] = acc
    xc = _silu(cv_ref[...])
    xcb = xc.astype(BF16)
    xmb = xm.astype(BF16)
    xc_ref[...] = xcb
    gates = jnp.broadcast_to(bg_ref[...], (tl, LANES))
    for s in range(c // MXU_DIM):
        sl = slice(MXU_DIM * s, MXU_DIM * (s + 1))
        qb = _dot(xcb[:, sl], bdq_ref[s]).astype(BF16)
        kf = _dot(xcb[:, sl], bdk_ref[s])
        kb = kf.astype(BF16)
        vb = _dot(xmb[:, sl], bdv_ref[s]).astype(BF16)
        gates = gates + _dot(qb, wgq_ref[sl, :]) + _dot(kb, wgk_ref[sl, :]) + _dot(vb, wgv_ref[sl, :])
        q_ref[:, sl] = qb
        k_ref[:, sl] = (kf * kscale).astype(BF16)
        v_ref[:, sl] = vb
    gates_ref[...] = gates


def _a_local(xm_src, seq, cw, cb, bdq, bdk, bdv, wgq, wgk, wgv, bg, kscale):
    t = xm_src.shape[0]
    c = cw.shape[1]
    tl = A_LOCAL_TILE
    nslab = c // MXU_DIM
    full = lambda shape: pl.BlockSpec(shape, lambda i: (0,) * len(shape))
    tok = lambda w: pl.BlockSpec((tl, w), lambda i: (i, 0))
    return pl.pallas_call(
        functools.partial(_a_local_body, tl=tl, tiles_per_seq=seq // tl, kscale=kscale),
        grid=(t // tl,),
        in_specs=[tok(c), full((A_CONV, c)), full((1, c)),
                  full((nslab, MXU_DIM, MXU_DIM)), full((nslab, MXU_DIM, MXU_DIM)),
                  full((nslab, MXU_DIM, MXU_DIM)),
                  full((c, LANES)), full((c, LANES)), full((c, LANES)), full((1, LANES))],
        out_specs=[tok(c), tok(c), tok(c), tok(c), tok(LANES)],
        out_shape=[jax.ShapeDtypeStruct((t, c), BF16)] * 4 + [jax.ShapeDtypeStruct((t, LANES), F32)],
        scratch_shapes=[pltpu.VMEM((tl + 2 * HALO, c), F32), pltpu.VMEM((tl, c), F32)],
        compiler_params=_params(1),
        name="a_local",
    )(xm_src, cw, cb, bdq, bdk, bdv, wgq, wgk, wgv, bg)


def _a_cell_body(q_ref, k_ref, v_ref, xc_ref, o_ref, z_ref, gates_ref, hn_ref, skip_ref,
                 y_ref, c_ref, m_ref, *, chunk, dh):
    cidx = pl.program_id(1)
    nh = A_HEADS

    @pl.when(cidx == 0)
    def _():
        c_ref[...] = jnp.zeros(c_ref.shape, F32)
        m_ref[...] = jnp.zeros(m_ref.shape, F32)

    gates = gates_ref[...]
    lf = jnp.minimum(gates, 0.0) - jnp.log(1.0 + jnp.exp(-jnp.abs(gates)))
    r_i = lax.broadcasted_iota(jnp.int32, (chunk, chunk), 0)
    c_i = lax.broadcasted_iota(jnp.int32, (chunk, chunk), 1)
    causal = r_i >= c_i
    tril = causal.astype(BF16)
    hi = lf.astype(BF16)
    r1 = lf - hi.astype(F32)
    mid = r1.astype(BF16)
    lo = (r1 - mid.astype(F32)).astype(BF16)
    b_col = _dot(tril, hi) + _dot(tril, mid) + _dot(tril, lo)
    g_t = gates.T
    b_t = b_col.T

    for h in range(nh):
        hs = slice(h * dh, (h + 1) * dh)
        qh = q_ref[:, hs]
        kh = k_ref[:, hs]
        vh = v_ref[:, hs]
        i_row = g_t[h:h + 1, :]
        b_row = b_t[nh + h:nh + h + 1, :]
        i_c = gates[:, h:h + 1]
        b_c = b_col[:, nh + h:nh + h + 1]
        m_b = m_ref[h:h + 1, 0:1]
        log_d = jnp.where(causal, b_c - b_row + i_row, -jnp.inf)
        m_inter = b_c + m_b
        m_t = jnp.maximum(jnp.max(log_d, axis=1, keepdims=True), m_inter)
        d = jnp.exp(log_d - m_t)
        s = _dot_nt(qh, kh) * d
        decay = jnp.exp(m_inter - m_t)
        inter = _dot(qh, c_ref[h].astype(BF16))
        intra = _dot(s.astype(BF16), vh)
        num = decay * inter[:, :dh] + intra
        den = decay * inter[:, dh:] + jnp.sum(s, axis=1, keepdims=True)
        scale = 1.0 / jnp.maximum(jnp.abs(den), jnp.exp(-m_t))
        b_last = b_col[chunk - 1:chunk, nh + h:nh + h + 1]
        m_new = jnp.maximum(b_last + m_b,
                            jnp.max(b_last - b_row + i_row, axis=1, keepdims=True))
        w_c = jnp.exp(b_last - b_c + i_c - m_new)
        carry = jnp.exp(b_last + m_b - m_new)
        wv = jnp.concatenate([vh.astype(F32) * w_c, jnp.broadcast_to(w_c, (chunk, LANES))],
                             axis=1).astype(BF16)
        c_ref[h] = carry * c_ref[h] + _dot_tn(kh, wv)
        m_ref[h:h + 1, :] = jnp.broadcast_to(m_new, (1, LANES))
        lanes = [slice(h * dh + j * LANES, h * dh + (j + 1) * LANES) for j in range(dh // LANES)]
        hc = [o_ref[:, ls] * (num[:, j * LANES:(j + 1) * LANES] * scale)
              for j, ls in enumerate(lanes)]
        mu = _tile_sum(hc) * (1.0 / dh)
        cen = [t - mu for t in hc]
        rstd = lax.rsqrt(_tile_sum([t * t for t in cen]) * (1.0 / dh) + EPS)
        for t, ls in zip(cen, lanes):
            y = (t * rstd * hn_ref[:, ls] + skip_ref[:, ls] * xc_ref[:, ls].astype(F32)) * z_ref[:, ls]
            y_ref[:, ls] = y.astype(BF16)


def _a_cell(q, k, v, xc, proj, gates, head_norm, skip, batch, seq):
    t, c = q.shape
    chunk = A_CHUNK
    nc = seq // chunk
    dh = c // A_HEADS
    tokc = lambda j: pl.BlockSpec((chunk, c), lambda b, i, j=j: (b * nc + i, j))
    full = lambda shape: pl.BlockSpec(shape, lambda b, i: (0,) * len(shape))
    return pl.pallas_call(
        functools.partial(_a_cell_body, chunk=chunk, dh=dh),
        grid=(batch, nc),
        in_specs=[tokc(0), tokc(0), tokc(0), tokc(0), tokc(1), tokc(2),
                  pl.BlockSpec((chunk, LANES), lambda b, i: (b * nc + i, 0)),
                  full((1, c)), full((1, c))],
        out_specs=tokc(0),
        out_shape=jax.ShapeDtypeStruct((t, c), BF16),
        scratch_shapes=[pltpu.VMEM((A_HEADS, dh, dh + LANES), F32),
                        pltpu.VMEM((SUBLANES, LANES), F32)],
        compiler_params=_params(2),
        name="a_cell",
    )(q, k, v, xc, proj, proj, gates, head_norm, skip)


def _kv_compress_body(x_ref, pa_ref, pb_ref, w1a_ref, w1b_ref, b1_ref, w2_ref, b2_ref,
                      o_ref, nxt_ref):
    x = x_ref[0, 0]
    nr = x.shape[0]
    first = _dot((x + pa_ref[0]).astype(BF16), w1a_ref[0])
    nxt_ref[0:nr, :] = _dot((x + pb_ref[0]).astype(BF16), w1b_ref[0])
    nxt_ref[nr:nr + SUBLANES, :] = jnp.zeros((SUBLANES, nxt_ref.shape[1]), F32)
    h1 = _silu(first + nxt_ref[1:nr + 1, :] + b1_ref[0])
    o_ref[0, 0] = (_dot(h1.astype(BF16), w2_ref[0]) + b2_ref[0]).astype(o_ref.dtype)


def _kv_compress(x16, pos_a, pos_b, w1a, w1b, b1, w2, b2):
    nkg, batch, nr, half = x16.shape
    hid = w1a.shape[2]
    dk = w2.shape[2]
    kind = lambda shape: pl.BlockSpec((1,) + shape, lambda p, b: (p // B_GROUPS, 0, 0))
    return pl.pallas_call(
        _kv_compress_body,
        grid=(nkg, batch),
        in_specs=[pl.BlockSpec((1, 1, nr, half), lambda p, b: (p, b, 0, 0)),
                  kind((1, half)), kind((1, half)),
                  kind((half, hid)), kind((half, hid)), kind((1, hid)),
                  kind((hid, dk)), kind((1, dk))],
        out_specs=pl.BlockSpec((1, 1, nr, dk), lambda p, b: (p, b, 0, 0)),
        out_shape=jax.ShapeDtypeStruct((nkg, batch, nr, dk), BF16),
        scratch_shapes=[pltpu.VMEM((nr + SUBLANES, hid), F32)],
        compiler_params=_params(2),
        name="kv_compress",
    )(x16, pos_a, pos_b, w1a, w1b, b1, w2, b2)


def _stack_heads(q):
    return jnp.concatenate([q[:, h * B_DK:(h + 1) * B_DK] for h in range(B_HPG)], axis=0)


LOG2E = 1.4426950408889634


def _head_slopes(g):
    return [LOG2E * jnp.exp2(jnp.full((1, LANES), -0.5, F32) * (g * B_HPG + h + 1).astype(F32))
            for h in range(B_HPG)]


def _lane_tiles(a):
    return [a[:, j * LANES:(j + 1) * LANES] for j in range(a.shape[1] // LANES)]


def _head_logits(s, h, tq, slope, rel, madd):
    rows = slice(h * tq, (h + 1) * tq)
    bias = [slope * r for r in _lane_tiles(rel)]
    if madd is not None:
        bias = [b_ + ma for b_, ma in zip(bias, _lane_tiles(madd))]
    return [s[rows, j * LANES:(j + 1) * LANES] + b_ for j, b_ in enumerate(bias)]


def _with_ones(v):
    return jnp.concatenate([v, jnp.ones_like(v)], axis=1)


def _tile_max(tiles):
    mx = tiles[0]
    for t in tiles[1:]:
        mx = jnp.maximum(mx, t)
    return jnp.max(mx, axis=1, keepdims=True)


def _tile_sum(tiles):
    sm = tiles[0]
    for t in tiles[1:]:
        sm = sm + t
    return jnp.sum(sm, axis=1, keepdims=True)


def _nsa_cmp_body(q_ref, kc_ref, vc_ref, zg_ref, gl_ref, wov_ref,
                  oc_ref, sel_ref, any_ref, o_scr, imp_scr, *, tq, n_sel):
    g = pl.program_id(1)
    t0 = pl.program_id(2) * tq
    ncp = kc_ref.shape[2]
    nsp = wov_ref.shape[0]

    def attend(width):
        qs = _stack_heads(q_ref[...])
        slopes = _head_slopes(g)
        n = lax.broadcasted_iota(jnp.int32, (1, width), 1)
        rel = ((n * CMP_STRIDE - t0).astype(F32) + 0.5 * (CMP_BLK - 1))
        c_end = lax.broadcasted_iota(jnp.int32, (tq, width), 1) * CMP_STRIDE + (CMP_BLK - 1)
        tok = t0 + lax.broadcasted_iota(jnp.int32, (tq, width), 0)
        madd = jnp.where(c_end <= tok, 0.0, NEG)
        s = _dot_nt(qs, kc_ref[0, 0, 0:width, :])
        p_rows = []
        psum_tiles = None
        for h in range(B_HPG):
            tiles = _head_logits(s, h, tq, slopes[h], rel, madd)
            m = jnp.maximum(_tile_max(tiles), M_INIT)
            es = [jnp.exp2(t - m) for t in tiles]
            inv = 1.0 / jnp.maximum(_tile_sum(es), 1e-30)
            ps = [e * inv for e in es]
            psum_tiles = ps if psum_tiles is None else [a + b_ for a, b_ in zip(psum_tiles, ps)]
            p_rows.append(jnp.concatenate([p.astype(BF16) for p in ps], axis=1))
        o_scr[...] = _dot(jnp.concatenate(p_rows, axis=0), vc_ref[0, 0, 0:width, :])
        psum = jnp.concatenate(psum_tiles, axis=1)
        p_hi = psum.astype(BF16)
        p_lo = (psum - p_hi.astype(F32)).astype(BF16)
        wov = wov_ref[:, 0:width]
        imp_scr[...] = _dot_nt(wov, p_hi) + _dot_nt(wov, p_lo)

    ntile = ncp // LANES
    need = _shr(t0 + tq - 1, CMP_STRIDE * LANES) + 1
    for k in range(1, ntile + 1):
        @pl.when((need == k) if k < ntile else (need >= k))
        def _(k=k):
            attend(k * LANES)

    o = o_scr[...]
    imp = imp_scr[...]
    j = lax.broadcasted_iota(jnp.int32, (nsp, tq), 0)
    cur = _shr(t0 + lax.broadcasted_iota(jnp.int32, (1, tq), 1), SEL_BLK)
    forced = (j == 0) | (j == cur) | (j == cur - 1)
    val = jnp.where(forced, jnp.inf, jnp.where(j <= cur, imp, -jnp.inf))
    sel_t = jnp.zeros((nsp, tq), F32)
    for _ in range(n_sel):
        mx = jnp.max(val, axis=0, keepdims=True)
        cand = jnp.where((val == mx) & (mx > -jnp.inf), j, nsp)
        pick = j == jnp.min(cand, axis=0, keepdims=True)
        sel_t = jnp.where(pick, 1.0, sel_t)
        val = jnp.where(pick, -jnp.inf, val)
    sel = sel_t.T
    sel_ref[0, 0] = sel.astype(sel_ref.dtype)
    any_ref[0, 0, 0] = jnp.broadcast_to(jnp.max(sel, axis=0, keepdims=True), (SUBLANES, nsp))
    gate = gl_ref[...]
    for h in range(B_HPG):
        hs = slice(h * B_DK, (h + 1) * B_DK)
        oc_ref[:, hs] = gate[:, h:h + 1] * zg_ref[:, hs] * o[h * tq:(h + 1) * tq]


def _nsa_cmp(q, kvc, zg, gates, wov_t, batch, seq, n_sel):
    t = q.shape[0]
    tq = Q_TILE
    nqb = seq // tq
    ncp = kvc.shape[2]
    nsp = wov_t.shape[0]
    gw = B_HPG * B_DK
    return pl.pallas_call(
        functools.partial(_nsa_cmp_body, tq=tq, n_sel=n_sel),
        grid=(batch, B_GROUPS, nqb),
        in_specs=[pl.BlockSpec((tq, gw), lambda b, g, i: (b * nqb + i, g)),
                  pl.BlockSpec((1, 1, ncp, B_DK), lambda b, g, i: (g, b, 0, 0)),
                  pl.BlockSpec((1, 1, ncp, B_DK), lambda b, g, i: (B_GROUPS + g, b, 0, 0)),
                  pl.BlockSpec((tq, gw), lambda b, g, i: (b * nqb + i, g)),
                  pl.BlockSpec((tq, LANES), lambda b, g, i: (b * nqb + i, g)),
                  pl.BlockSpec((nsp, ncp), lambda b, g, i: (0, 0))],
        out_specs=[pl.BlockSpec((tq, gw), lambda b, g, i: (b * nqb + i, g)),
                   pl.BlockSpec((1, 1, tq, nsp), lambda b, g, i: (b, g, i, 0)),
                   pl.BlockSpec((1, 1, 1, SUBLANES, nsp), lambda b, g, i: (b, g, i, 0, 0))],
        out_shape=[jax.ShapeDtypeStruct((t, B_GROUPS * gw), F32),
                   jax.ShapeDtypeStruct((batch, B_GROUPS, seq, nsp), BF16),
                   jax.ShapeDtypeStruct((batch, B_GROUPS, nqb, SUBLANES, nsp), F32)],
        scratch_shapes=[pltpu.VMEM((B_HPG * tq, B_DK), F32), pltpu.VMEM((nsp, tq), F32)],
        compiler_params=_params(3),
        name="nsa_cmp",
    )(q, kvc, kvc, zg, gates, wov_t)


def _softmax_once(s, v, slopes, rel, madd, tq):
    ms = [jnp.maximum(_tile_max(_head_logits(s, h, tq, slopes[h], rel, madd)), M_INIT)
          for h in range(B_HPG)]
    p_rows = []
    for h in range(B_HPG):
        tiles = _head_logits(s, h, tq, slopes[h], rel, madd)
        p_rows.append(jnp.concatenate([jnp.exp2(t - ms[h]).astype(BF16) for t in tiles], axis=1))
    pv = _dot(jnp.concatenate(p_rows, axis=0), _with_ones(v))
    return pv[:, :B_DK] * (1.0 / jnp.maximum(pv[:, B_DK:], 1e-30))


def _online_update(s, v_c, slopes, rel, madd, m_ref, l_ref, acc_ref, tq):
    p_rows = []
    alphas = []
    m_news = []
    for h in range(B_HPG):
        rows = slice(h * tq, (h + 1) * tq)
        m_prev = m_ref[rows, :]
        m_new = jnp.maximum(m_prev, _tile_max(_head_logits(s, h, tq, slopes[h], rel, madd)))
        m_ref[rows, :] = m_new
        alphas.append(jnp.exp2(m_prev - m_new))
        m_news.append(m_new)
    for h in range(B_HPG):
        tiles = _head_logits(s, h, tq, slopes[h], rel, madd)
        p_rows.append(jnp.concatenate([jnp.exp2(t - m_news[h]).astype(BF16) for t in tiles], axis=1))
    pv = _dot(jnp.concatenate(p_rows, axis=0), _with_ones(v_c))
    for h in range(B_HPG):
        rows = slice(h * tq, (h + 1) * tq)
        acc_ref[rows, :] = alphas[h] * acc_ref[rows, :] + pv[rows, :B_DK]
        l_ref[rows, :] = alphas[h] * l_ref[rows, :] + pv[rows, B_DK:]


def _softmax_reset(m_ref, l_ref, acc_ref):
    m_ref[...] = jnp.full(m_ref.shape, M_INIT, F32)
    l_ref[...] = jnp.zeros(l_ref.shape, F32)
    acc_ref[...] = jnp.zeros(acc_ref.shape, F32)


def _softmax_result(l_ref, acc_ref):
    return acc_ref[...] * (1.0 / jnp.maximum(l_ref[...], 1e-30))


def _nsa_sw_body(lst_ref, q_ref, sel_ref, ks_ref, vs_ref, kw_ref, vw_ref,
                 zs_ref, zw_ref, gl_ref, oc_ref, out_ref,
                 qs_ref, s_a, s_b, sw_ref, m_ref, l_ref, acc_ref,
                 *, tq, nqb, nch):
    b = pl.program_id(0)
    g = pl.program_id(1)
    i = pl.program_id(2)
    t0 = i * tq
    qs_ref[:, :B_DK] = _stack_heads(q_ref[...])
    not_sel = (1.0 - sel_ref[0, 0].astype(F32)).astype(BF16)
    qs_ref[:, B_DK:] = jnp.concatenate([not_sel] * B_HPG, axis=0)
    slopes = _head_slopes(g)
    nsp = sel_ref.shape[3]
    ch = SEL_CHUNK
    bpc = ch // SEL_BLK
    lbase = ((b * B_GROUPS + g) * nqb + i) * (nch + 2)
    n_reg = lst_ref[lbase] - 1

    def positions(start, width):
        rel = (start - t0) + lax.broadcasted_iota(jnp.int32, (1, width), 1)
        dist = lax.broadcasted_iota(jnp.int32, (tq, width), 0) - (
            (start - t0) + lax.broadcasted_iota(jnp.int32, (tq, width), 1))
        return rel.astype(F32), dist

    def scores(c):
        start = pl.multiple_of(c * ch, ch)
        key_blk = c * bpc + _shr(lax.broadcasted_iota(jnp.int32, (ch, nsp), 0), SEL_BLK)
        own_blk = jnp.where(lax.broadcasted_iota(jnp.int32, (ch, nsp), 1) == key_blk, MASK_NEG, 0.0)
        keys = jnp.concatenate([ks_ref[0, 0, pl.ds(start, ch), :], own_blk.astype(BF16)], axis=1)
        return _dot_nt(qs_ref[...], keys)

    def stage(j, s_ref):
        s_ref[...] = scores(lst_ref[lbase + 1 + j])

    def consume(j, s_ref):
        start = pl.multiple_of(lst_ref[lbase + 1 + j] * ch, ch)
        rel, _ = positions(start, ch)
        _online_update(s_ref, vs_ref[0, 0, pl.ds(start, ch), :], slopes, rel, None,
                       m_ref, l_ref, acc_ref, tq)

    def consume_own(s_ref):
        start = pl.multiple_of(lst_ref[lbase + 1 + n_reg] * ch, ch)
        rel, dist = positions(start, ch)
        _online_update(s_ref, vs_ref[0, 0, pl.ds(start, ch), :], slopes, rel,
                       jnp.where(dist >= 0, 0.0, NEG), m_ref, l_ref, acc_ref, tq)

    nwin = WINDOW // ch + max(tq // ch, 1)
    c_lo = jnp.maximum(_shr(t0 + tq - 1, ch) - (nwin - 1), 0)
    st = pl.multiple_of(c_lo * ch, ch)
    sw_ref[...] = _dot_nt(qs_ref[:, :B_DK], kw_ref[0, 0, pl.ds(st, nwin * ch), :])

    _softmax_reset(m_ref, l_ref, acc_ref)
    stage(0, s_a)

    def chunk_pair(jj, carry):
        j = 2 * jj
        stage(j + 1, s_b)
        consume(j, s_a)

        @pl.when(j + 1 < n_reg)
        def _():
            stage(j + 2, s_a)
            consume(j + 1, s_b)
        return carry

    lax.fori_loop(0, _shr(n_reg + 1, 2), chunk_pair, 0)

    @pl.when((n_reg & 1) == 0)
    def _():
        consume_own(s_a)

    @pl.when((n_reg & 1) == 1)
    def _():
        consume_own(s_b)

    o_s = _softmax_result(l_ref, acc_ref)

    rel, dist = positions(st, nwin * ch)
    madd = jnp.where(dist >= 0, jnp.where(dist < WINDOW, 0.0, NEG), NEG)
    o_w = _softmax_once(sw_ref, vw_ref[0, 0, pl.ds(st, nwin * ch), :], slopes, rel, madd, tq)

    gate = gl_ref[...]
    for h in range(B_HPG):
        hs = slice(h * B_DK, (h + 1) * B_DK)
        rs = slice(h * tq, (h + 1) * tq)
        merged = (oc_ref[:, hs]
                  + gate[:, B_HPG + h:B_HPG + h + 1] * zs_ref[:, hs] * o_s[rs]
                  + gate[:, 2 * B_HPG + h:2 * B_HPG + h + 1] * zw_ref[:, hs] * o_w[rs])
        out_ref[:, hs] = merged.astype(out_ref.dtype)


def _nsa_sw(flags, q, sel, kv_att, zg, gates, oc, batch, seq):
    t = q.shape[0]
    tq = Q_TILE
    nqb = seq // tq
    nch = seq // SEL_CHUNK
    nsp = sel.shape[3]
    gw = B_HPG * B_DK
    rows = B_HPG * tq
    tok = lambda j: pl.BlockSpec((tq, gw), lambda b, g, i, f, j=j: (b * nqb + i, j * B_GROUPS + g))
    res = lambda k: pl.BlockSpec((1, 1, seq, B_DK), lambda b, g, i, f, k=k: (k * B_GROUPS + g, b, 0, 0))
    grid_spec = pltpu.PrefetchScalarGridSpec(
        num_scalar_prefetch=1,
        grid=(batch, B_GROUPS, nqb),
        in_specs=[tok(0),
                  pl.BlockSpec((1, 1, tq, nsp), lambda b, g, i, f: (b, g, i, 0)),
                  res(0), res(1), res(2), res(3),
                  tok(1), tok(2),
                  pl.BlockSpec((tq, LANES), lambda b, g, i, f: (b * nqb + i, g)),
                  tok(0)],
        out_specs=tok(0),
        scratch_shapes=[pltpu.VMEM((rows, B_DK + nsp), BF16),
                        pltpu.VMEM((rows, SEL_CHUNK), F32),
                        pltpu.VMEM((rows, SEL_CHUNK), F32),
                        pltpu.VMEM((rows, WINDOW + max(tq, SEL_CHUNK)), F32),
                        pltpu.VMEM((rows, LANES), F32),
                        pltpu.VMEM((rows, LANES), F32),
                        pltpu.VMEM((rows, B_DK), F32)])
    return pl.pallas_call(
        functools.partial(_nsa_sw_body, tq=tq, nqb=nqb, nch=nch),
        grid_spec=grid_spec,
        out_shape=jax.ShapeDtypeStruct((t, B_GROUPS * gw), BF16),
        compiler_params=_params(3),
        name="nsa_sw",
    )(flags, q, sel, kv_att, kv_att, kv_att, kv_att, zg, zg, gates, oc)


def _blockdiag_dense(w):
    nb, bs, _ = w.shape
    per = MXU_DIM // bs
    w4 = w.reshape(nb // per, per, bs, bs)
    dense = jnp.einsum("sgij,gh->sgihj", w4, jnp.eye(per, dtype=w.dtype))
    return dense.reshape(nb // per, MXU_DIM, MXU_DIM).astype(BF16)


def _pad_cols(w, width):
    return jnp.pad(w, ((0, 0), (0, width - w.shape[1])))


def _overlap_matrix_t(nr, nsp, nc, ns):
    i = jnp.arange(nr)[None, :] * CMP_STRIDE
    j = jnp.arange(nsp)[:, None] * SEL_BLK
    ov = jnp.minimum(i + CMP_BLK, j + SEL_BLK) - jnp.maximum(i, j)
    ov = jnp.maximum(ov, 0) / CMP_STRIDE
    live = (jnp.arange(nr)[None, :] < nc) & (jnp.arange(nsp)[:, None] < ns)
    return jnp.where(live, ov, 0).astype(BF16)


def _mlstm_layer(xf, batch, seq, g_pre, g_post, w_in, conv_w, conv_b, w_q, w_k, w_v,
                 w_gate, b_gate, head_norm, skip, w_out):
    c = conv_w.shape[1]
    dh = c // A_HEADS
    assert w_in.shape[1] == 3 * c
    proj = _rms_matmul(xf, g_pre, w_in.astype(BF16), F32, acts=(None, "sigmoid", "silu"))
    wg = [_pad_cols(w_gate[j * c:(j + 1) * c], LANES).astype(BF16) for j in range(3)]
    bg = _pad_cols(b_gate.reshape(1, -1), LANES).astype(F32)
    q, k, v, xc, gates = _a_local(
        proj, seq, conv_w.astype(F32), conv_b.reshape(1, c).astype(F32),
        _blockdiag_dense(w_q), _blockdiag_dense(w_k), _blockdiag_dense(w_v),
        wg[0], wg[1], wg[2], bg, float(dh) ** -0.5)
    y = _a_cell(q, k, v, xc, proj, gates, head_norm.reshape(1, c).astype(F32),
                skip.reshape(1, c).astype(F32), batch, seq)
    return _out_proj(y, w_out.astype(BF16), g_post, xf)


def _shared_kv(xf, batch, seq, kv_norm, kv_w, cmp_pos, cmp_w1, cmp_b1, cmp_w2, cmp_b2):
    ncmp = 2 * B_GROUPS * B_DK
    kv_w = kv_w.astype(BF16)
    kv_cmp, h = _rms_matmul(xf, kv_norm, kv_w[:, :ncmp], F32, split=True, emit_h=True)
    kv_att = _matmul(h, kv_w[:, ncmp:], BF16, split=True)
    nr = seq // CMP_STRIDE
    half = CMP_STRIDE * B_DK
    pos_a = cmp_pos[:, :CMP_STRIDE].reshape(2, 1, half).astype(F32)
    pos_b = cmp_pos[:, CMP_STRIDE:].reshape(2, 1, half).astype(F32)
    kvc = _kv_compress(kv_cmp.reshape(2 * B_GROUPS, batch, nr, half), pos_a, pos_b,
                       cmp_w1[:, :half].astype(BF16), cmp_w1[:, half:].astype(BF16),
                       cmp_b1.reshape(2, 1, -1).astype(F32), cmp_w2.astype(BF16),
                       cmp_b2.reshape(2, 1, -1).astype(F32))
    return kvc, kv_att.reshape(4 * B_GROUPS, batch, seq, B_DK)


def _nsa_layer(xf, batch, seq, g_pre, g_post, w_in, w_out, shared, wov_t, n_sel):
    kvc, kv_att = shared
    nqk = B_HEADS * B_DK
    ngl = N_BRANCH * B_HEADS
    w_q = w_in[:, :nqk].astype(BF16)
    w_gl = w_in[:, nqk:nqk + ngl].reshape(-1, N_BRANCH, B_GROUPS, B_HPG)
    w_gl = jnp.transpose(w_gl, (0, 2, 1, 3)).reshape(-1, B_GROUPS, N_BRANCH * B_HPG)
    w_gl = jnp.pad(w_gl, ((0, 0), (0, 0), (0, LANES - N_BRANCH * B_HPG))).reshape(-1, B_GROUPS * LANES)
    q, h = _rms_matmul(xf, g_pre, w_q, BF16, scale=LOG2E * float(B_DK) ** -0.5, emit_h=True)
    zg = _matmul(h, w_in[:, nqk + ngl:].astype(BF16), F32, act="silu")
    gates = _matmul(h, w_gl.astype(BF16), F32, act="sigmoid")
    oc, sel, blk_any = _nsa_cmp(q, kvc, zg, gates, wov_t, batch, seq, n_sel)
    ns = seq // SEL_BLK
    bpc = SEL_CHUNK // SEL_BLK
    flags = blk_any[:, :, :, 0, :ns].reshape(batch, B_GROUPS, seq // Q_TILE, ns // bpc, bpc)
    flags = (jnp.max(flags, axis=-1) > 0).astype(jnp.int32)
    order = jnp.argsort(1 - flags, axis=-1, stable=True).astype(jnp.int32)
    count = jnp.sum(flags, axis=-1, keepdims=True)
    lists = jnp.concatenate([count, order, jnp.zeros_like(count)], axis=-1).reshape(-1)
    out = _nsa_sw(lists, q, sel, kv_att, zg, gates, oc, batch, seq)
    return _out_proj(out, w_out.astype(BF16), g_post, xf)


def kernel(x, norm_pre, norm_post, a_w_in, a_conv_w, a_conv_b, a_w_q, a_w_k, a_w_v, a_w_gate,
           a_b_gate, a_head_norm, a_skip, a_w_out, kv_norm, kv_w, cmp_pos, cmp_w1, cmp_b1,
           cmp_w2, cmp_b2, b_w_in, b_w_out):
    batch, seq, d = x.shape
    na = a_w_in.shape[0]
    nb = b_w_in.shape[0]
    assert seq % (CMP_STRIDE * LANES) == 0 and seq % SEL_CHUNK == 0 and seq % A_CHUNK == 0
    xf = x.reshape(batch * seq, d).astype(F32)
    for l in range(na):
        xf = _mlstm_layer(xf, batch, seq, norm_pre[l], norm_post[l], a_w_in[l], a_conv_w[l],
                          a_conv_b[l], a_w_q[l], a_w_k[l], a_w_v[l], a_w_gate[l], a_b_gate[l],
                          a_head_norm[l], a_skip[l], a_w_out[l])
    shared = _shared_kv(xf, batch, seq, kv_norm, kv_w, cmp_pos, cmp_w1, cmp_b1, cmp_w2, cmp_b2)
    nr = seq // CMP_STRIDE
    ns = seq // SEL_BLK
    nsp = -(-ns // LANES) * LANES
    wov_t = _overlap_matrix_t(nr, nsp, nr - CMP_BLK // CMP_STRIDE + 1, ns)
    for l in range(nb):
        xf = _nsa_layer(xf, batch, seq, norm_pre[na + l], norm_post[na + l], b_w_in[l],
                        b_w_out[l], shared, wov_t, min(SEL_TOP, ns))
    return xf.reshape(batch, seq, d).astype(x.dtype)
```

```python
import functools

import jax
import jax.numpy as jnp
from jax import lax
from jax.experimental import pallas as pl
from jax.experimental.pallas import tpu as pltpu

F32 = jnp.float32
BF16 = jnp.bfloat16
EPS = 1e-6

A_HEADS = 4
A_CONV = 4
B_HEADS = 16
B_GROUPS = 2
B_HPG = B_HEADS // B_GROUPS
B_DK = 128
N_BRANCH = 3
CMP_BLK = 32
CMP_STRIDE = 16
CMP_HIDDEN = 256
SEL_BLK = 64
SEL_TOP = 16
WINDOW = 512

LANES = 128
SUBLANES = 8
MXU_DIM = 256
VMEM_LIMIT_BYTES = 56 * 1024 * 1024

A_CHUNK = 256
A_LOCAL_TILE = 256
Q_TILE = 256
SEL_CHUNK = 256
NEG = -1e30
M_INIT = -1e20
MASK_NEG = -2.0 ** 100
assert B_DK == LANES


def _params(n_axes):
    return pltpu.CompilerParams(
        dimension_semantics=("arbitrary",) * n_axes,
        vmem_limit_bytes=VMEM_LIMIT_BYTES)


def _sigmoid(v):
    return 1.0 / (1.0 + jnp.exp(-v))


def _silu(v):
    return v * _sigmoid(v)


def _shr(v, pow2):
    assert pow2 & (pow2 - 1) == 0
    return lax.shift_right_logical(v, jnp.int32(pow2.bit_length() - 1))


def _dot(a, b):
    return jnp.dot(a, b, preferred_element_type=F32)


def _dot_nt(a, b):
    return lax.dot_general(a, b, (((1,), (1,)), ((), ())), preferred_element_type=F32)


def _dot_tn(a, b):
    return lax.dot_general(a, b, (((0,), (0,)), ((), ())), preferred_element_type=F32)


_ACTS = {None: lambda v: v, "sigmoid": _sigmoid, "silu": _silu}


def _store_proj(o_ref, acc, scale, acts):
    if scale != 1.0:
        acc = acc * scale

    def store(act):
        val = _ACTS[act](acc)
        if len(o_ref.shape) == 2:
            o_ref[...] = val.astype(o_ref.dtype)
        else:
            for j in range(o_ref.shape[0]):
                o_ref[j] = val[:, j * LANES:(j + 1) * LANES].astype(o_ref.dtype)

    if len(acts) == 1:
        store(acts[0])
    else:
        for idx, act in enumerate(acts):
            pl.when(pl.program_id(0) == idx)(functools.partial(store, act))


def _rms_matmul_body(x_ref, g_ref, w_ref, o_ref, *h_ref, scale, acts):
    x = x_ref[...]
    ms = jnp.mean(x * x, axis=-1, keepdims=True)
    h = ((x * lax.rsqrt(ms + EPS)) * g_ref[...]).astype(BF16)
    if h_ref:
        h_ref[0][...] = h
    _store_proj(o_ref, _dot(h, w_ref[...]), scale, acts)


def _matmul_body(h_ref, w_ref, o_ref, *, acts):
    _store_proj(o_ref, _dot(h_ref[...], w_ref[...]), 1.0, acts)


def _pick_tile(n, candidates):
    for c in candidates:
        if n % c == 0:
            return c
    return n


def _proj_tiles(t, n, split, out_dtype):
    tm = _pick_tile(t, (1024, 512, 256, 128))
    tn = _pick_tile(n, (2048, 1536, 1280, 1024, 896, 768, 512, 256, 128))
    if split:
        out_spec = pl.BlockSpec((tn // LANES, tm, LANES), lambda j, i: (j, i, 0))
        out_shape = jax.ShapeDtypeStruct((n // LANES, t, LANES), out_dtype)
    else:
        out_spec = pl.BlockSpec((tm, tn), lambda j, i: (i, j))
        out_shape = jax.ShapeDtypeStruct((t, n), out_dtype)
    return tm, tn, out_spec, out_shape


def _rms_matmul(x, g, w, out_dtype, scale=1.0, split=False, emit_h=False, acts=(None,)):
    t, d = x.shape
    n = w.shape[1]
    tm, tn, out_spec, out_shape = _proj_tiles(t, n, split, out_dtype)
    assert len(acts) in (1, n // tn)
    out_specs, out_shapes = [out_spec], [out_shape]
    if emit_h:
        assert n == tn
        out_specs.append(pl.BlockSpec((tm, d), lambda j, i: (i, 0)))
        out_shapes.append(jax.ShapeDtypeStruct((t, d), BF16))
    res = pl.pallas_call(
        functools.partial(_rms_matmul_body, scale=scale, acts=acts),
        grid=(n // tn, t // tm),
        in_specs=[pl.BlockSpec((tm, d), lambda j, i: (i, 0)),
                  pl.BlockSpec((1, d), lambda j, i: (0, 0)),
                  pl.BlockSpec((d, tn), lambda j, i: (0, j))],
        out_specs=out_specs,
        out_shape=out_shapes,
        compiler_params=_params(2),
        name="rms_matmul",
    )(x, g.reshape(1, d).astype(F32), w)
    return res if emit_h else res[0]


def _matmul(h, w, out_dtype, split=False, act=None):
    t, d = h.shape
    n = w.shape[1]
    tm, tn, out_spec, out_shape = _proj_tiles(t, n, split, out_dtype)
    return pl.pallas_call(
        functools.partial(_matmul_body, acts=(act,)),
        grid=(n // tn, t // tm),
        in_specs=[pl.BlockSpec((tm, d), lambda j, i: (i, 0)),
                  pl.BlockSpec((d, tn), lambda j, i: (0, j))],
        out_specs=out_spec,
        out_shape=out_shape,
        compiler_params=_params(2),
        name="matmul",
    )(h, w)


def _out_proj_body(y_ref, w_ref, g_ref, x_ref, o_ref):
    r = _dot(y_ref[...], w_ref[...])
    ms = jnp.mean(r * r, axis=-1, keepdims=True)
    o_ref[...] = x_ref[...] + (r * lax.rsqrt(ms + EPS)) * g_ref[...]


def _out_proj(y, w, g, x):
    t, k = y.shape
    d = w.shape[1]
    tm = _pick_tile(t, (512, 256, 128))
    return pl.pallas_call(
        _out_proj_body,
        grid=(t // tm,),
        in_specs=[pl.BlockSpec((tm, k), lambda i: (i, 0)),
                  pl.BlockSpec((k, d), lambda i: (0, 0)),
                  pl.BlockSpec((1, d), lambda i: (0, 0)),
                  pl.BlockSpec((tm, d), lambda i: (i, 0))],
        out_specs=pl.BlockSpec((tm, d), lambda i: (i, 0)),
        out_shape=jax.ShapeDtypeStruct((t, d), F32),
        compiler_params=_params(1),
        name="out_proj",
    )(y, w, g.reshape(1, d).astype(F32), x)


HALO = SUBLANES


def _a_local_body(xm_ref, cw_ref, cb_ref, bdq_ref, bdk_ref, bdv_ref,
                  wgq_ref, wgk_ref, wgv_ref, bg_ref,
                  q_ref, k_ref, v_ref, xc_ref, gates_ref, ext_ref, cv_ref,
                  *, tl, tiles_per_seq, kscale):
    i = pl.program_id(0)
    c = xm_ref.shape[1]
    first = lax.rem(i, tiles_per_seq) == 0

    @pl.when(first)
    def _():
        ext_ref[0:HALO, :] = jnp.zeros((HALO, c), F32)

    @pl.when(jnp.logical_not(first))
    def _():
        ext_ref[0:HALO, :] = ext_ref[tl:tl + HALO, :]

    xm = xm_ref[...]
    ext_ref[HALO:HALO + tl, :] = xm
    acc = jnp.broadcast_to(cb_ref[...], (tl, c))
    for j in range(A_CONV):
        off = HALO - (A_CONV - 1) + j
        acc = acc + cw_ref[j:j + 1, :] * ext_ref[off:off + tl, :]
    cv_ref[...] = acc
    xc = _silu(cv_ref[...])
    xcb = xc.astype(BF16)
    xmb = xm.astype(BF16)
    xc_ref[...] = xcb
    gates = jnp.broadcast_to(bg_ref[...], (tl, LANES))
    for s in range(c // MXU_DIM):
        sl = slice(MXU_DIM * s, MXU_DIM * (s + 1))
        qb = _dot(xcb[:, sl], bdq_ref[s]).astype(BF16)
        kf = _dot(xcb[:, sl], bdk_ref[s])
        kb = kf.astype(BF16)
        vb = _dot(xmb[:, sl], bdv_ref[s]).astype(BF16)
        gates = gates + _dot(qb, wgq_ref[sl, :]) + _dot(kb, wgk_ref[sl, :]) + _dot(vb, wgv_ref[sl, :])
        q_ref[:, sl] = qb
        k_ref[:, sl] = (kf * kscale).astype(BF16)
        v_ref[:, sl] = vb
    gates_ref[...] = gates


def _a_local(xm_src, seq, cw, cb, bdq, bdk, bdv, wgq, wgk, wgv, bg, kscale):
    t = xm_src.shape[0]
    c = cw.shape[1]
    tl = A_LOCAL_TILE
    nslab = c // MXU_DIM
    full = lambda shape: pl.BlockSpec(shape, lambda i: (0,) * len(shape))
    tok = lambda w: pl.BlockSpec((tl, w), lambda i: (i, 0))
    return pl.pallas_call(
        functools.partial(_a_local_body, tl=tl, tiles_per_seq=seq // tl, kscale=kscale),
        grid=(t // tl,),
        in_specs=[tok(c), full((A_CONV, c)), full((1, c)),
                  full((nslab, MXU_DIM, MXU_DIM)), full((nslab, MXU_DIM, MXU_DIM)),
                  full((nslab, MXU_DIM, MXU_DIM)),
                  full((c, LANES)), full((c, LANES)), full((c, LANES)), full((1, LANES))],
        out_specs=[tok(c), tok(c), tok(c), tok(c), tok(LANES)],
        out_shape=[jax.ShapeDtypeStruct((t, c), BF16)] * 4 + [jax.ShapeDtypeStruct((t, LANES), F32)],
        scratch_shapes=[pltpu.VMEM((tl + 2 * HALO, c), F32), pltpu.VMEM((tl, c), F32)],
        compiler_params=_params(1),
        name="a_local",
    )(xm_src, cw, cb, bdq, bdk, bdv, wgq, wgk, wgv, bg)


def _a_cell_body(q_ref, k_ref, v_ref, xc_ref, o_ref, z_ref, gates_ref, hn_ref, skip_ref,
                 y_ref, c_ref, m_ref, *, chunk, dh):
    cidx = pl.program_id(1)
    nh = A_HEADS

    @pl.when(cidx == 0)
    def _():
        c_ref[...] = jnp.zeros(c_ref.shape, F32)
        m_ref[...] = jnp.zeros(m_ref.shape, F32)

    gates = gates_ref[...]
    lf = jnp.minimum(gates, 0.0) - jnp.log(1.0 + jnp.exp(-jnp.abs(gates)))
    r_i = lax.broadcasted_iota(jnp.int32, (chunk, chunk), 0)
    c_i = lax.broadcasted_iota(jnp.int32, (chunk, chunk), 1)
    causal = r_i >= c_i
    tril = causal.astype(BF16)
    hi = lf.astype(BF16)
    r1 = lf - hi.astype(F32)
    mid = r1.astype(BF16)
    lo = (r1 - mid.astype(F32)).astype(BF16)
    b_col = _dot(tril, hi) + _dot(tril, mid) + _dot(tril, lo)
    g_t = gates.T
    b_t = b_col.T

    for h in range(nh):
        hs = slice(h * dh, (h + 1) * dh)
        qh = q_ref[:, hs]
        kh = k_ref[:, hs]
        vh = v_ref[:, hs]
        i_row = g_t[h:h + 1, :]
        b_row = b_t[nh + h:nh + h + 1, :]
        i_c = gates[:, h:h + 1]
        b_c = b_col[:, nh + h:nh + h + 1]
        m_b = m_ref[h:h + 1, 0:1]
        log_d = jnp.where(causal, b_c - b_row + i_row, -jnp.inf)
        m_inter = b_c + m_b
        m_t = jnp.maximum(jnp.max(log_d, axis=1, keepdims=True), m_inter)
        d = jnp.exp(log_d - m_t)
        s = _dot_nt(qh, kh) * d
        decay = jnp.exp(m_inter - m_t)
        inter = _dot(qh, c_ref[h].astype(BF16))
        intra = _dot(s.astype(BF16), vh)
        num = decay * inter[:, :dh] + intra
        den = decay * inter[:, dh:] + jnp.sum(s, axis=1, keepdims=True)
        scale = 1.0 / jnp.maximum(jnp.abs(den), jnp.exp(-m_t))
        b_last = b_col[chunk - 1:chunk, nh + h:nh + h + 1]
        m_new = jnp.maximum(b_last + m_b,
                            jnp.max(b_last - b_row + i_row, axis=1, keepdims=True))
        w_c = jnp.exp(b_last - b_c + i_c - m_new)
        carry = jnp.exp(b_last + m_b - m_new)
        wv = jnp.concatenate([vh.astype(F32) * w_c, jnp.broadcast_to(w_c, (chunk, LANES))],
                             axis=1).astype(BF16)
        c_ref[h] = carry * c_ref[h] + _dot_tn(kh, wv)
        m_ref[h:h + 1, :] = jnp.broadcast_to(m_new, (1, LANES))
        lanes = [slice(h * dh + j * LANES, h * dh + (j + 1) * LANES) for j in range(dh // LANES)]
        hc = [_sigmoid(o_ref[:, ls]) * (num[:, j * LANES:(j + 1) * LANES] * scale)
              for j, ls in enumerate(lanes)]
        mu = _tile_sum(hc) * (1.0 / dh)
        cen = [t - mu for t in hc]
        rstd = lax.rsqrt(_tile_sum([t * t for t in cen]) * (1.0 / dh) + EPS)
        for t, ls in zip(cen, lanes):
            y = (t * rstd * hn_ref[:, ls] + skip_ref[:, ls] * xc_ref[:, ls].astype(F32)) * _silu(z_ref[:, ls])
            y_ref[:, ls] = y.astype(BF16)


def _a_cell(q, k, v, xc, proj, gates, head_norm, skip, batch, seq):
    t, c = q.shape
    chunk = A_CHUNK
    nc = seq // chunk
    dh = c // A_HEADS
    tokc = lambda j: pl.BlockSpec((chunk, c), lambda b, i, j=j: (b * nc + i, j))
    full = lambda shape: pl.BlockSpec(shape, lambda b, i: (0,) * len(shape))
    return pl.pallas_call(
        functools.partial(_a_cell_body, chunk=chunk, dh=dh),
        grid=(batch, nc),
        in_specs=[tokc(0), tokc(0), tokc(0), tokc(0), tokc(1), tokc(2),
                  pl.BlockSpec((chunk, LANES), lambda b, i: (b * nc + i, 0)),
                  full((1, c)), full((1, c))],
        out_specs=tokc(0),
        out_shape=jax.ShapeDtypeStruct((t, c), BF16),
        scratch_shapes=[pltpu.VMEM((A_HEADS, dh, dh + LANES), F32),
                        pltpu.VMEM((SUBLANES, LANES), F32)],
        compiler_params=_params(2),
        name="a_cell",
    )(q, k, v, xc, proj, proj, gates, head_norm, skip)


def _kv_compress_body(x_ref, pa_ref, pb_ref, w1a_ref, w1b_ref, b1_ref, w2_ref, b2_ref,
                      o_ref, nxt_ref):
    x = x_ref[0, 0]
    nr = x.shape[0]
    first = _dot((x + pa_ref[0]).astype(BF16), w1a_ref[0])
    nxt_ref[0:nr, :] = _dot((x + pb_ref[0]).astype(BF16), w1b_ref[0])
    nxt_ref[nr:nr + SUBLANES, :] = jnp.zeros((SUBLANES, nxt_ref.shape[1]), F32)
    h1 = _silu(first + nxt_ref[1:nr + 1, :] + b1_ref[0])
    o_ref[0, 0] = (_dot(h1.astype(BF16), w2_ref[0]) + b2_ref[0]).astype(o_ref.dtype)


def _kv_compress(x16, pos_a, pos_b, w1a, w1b, b1, w2, b2):
    nkg, batch, nr, half = x16.shape
    hid = w1a.shape[2]
    dk = w2.shape[2]
    kind = lambda shape: pl.BlockSpec((1,) + shape, lambda p, b: (p // B_GROUPS, 0, 0))
    return pl.pallas_call(
        _kv_compress_body,
        grid=(nkg, batch),
        in_specs=[pl.BlockSpec((1, 1, nr, half), lambda p, b: (p, b, 0, 0)),
                  kind((1, half)), kind((1, half)),
                  kind((half, hid)), kind((half, hid)), kind((1, hid)),
                  kind((hid, dk)), kind((1, dk))],
        out_specs=pl.BlockSpec((1, 1, nr, dk), lambda p, b: (p, b, 0, 0)),
        out_shape=jax.ShapeDtypeStruct((nkg, batch, nr, dk), BF16),
        scratch_shapes=[pltpu.VMEM((nr + SUBLANES, hid), F32)],
        compiler_params=_params(2),
        name="kv_compress",
    )(x16, pos_a, pos_b, w1a, w1b, b1, w2, b2)


def _stack_heads(q):
    return jnp.concatenate([q[:, h * B_DK:(h + 1) * B_DK] for h in range(B_HPG)], axis=0)


LOG2E = 1.4426950408889634


def _head_slopes(g):
    return [LOG2E * jnp.exp2(jnp.full((1, LANES), -0.5, F32) * (g * B_HPG + h + 1).astype(F32))
            for h in range(B_HPG)]


def _lane_tiles(a):
    return [a[:, j * LANES:(j + 1) * LANES] for j in range(a.shape[1] // LANES)]


def _head_logits(s, h, tq, slope, rel, madd):
    rows = slice(h * tq, (h + 1) * tq)
    tiles = [s[rows, j * LANES:(j + 1) * LANES] for j in range(s.shape[1] // LANES)]
    if slope is None and madd is None:
        return tiles
    if slope is None:
        bias = _lane_tiles(madd)
    else:
        bias = [slope * r for r in _lane_tiles(rel)]
        if madd is not None:
            bias = [b_ + ma for b_, ma in zip(bias, _lane_tiles(madd))]
    return [t + b_ for t, b_ in zip(tiles, bias)]


def _with_ones(v):
    return jnp.concatenate([v, jnp.ones_like(v)], axis=1)


def _tile_max(tiles):
    mx = tiles[0]
    for t in tiles[1:]:
        mx = jnp.maximum(mx, t)
    return jnp.max(mx, axis=1, keepdims=True)


def _tile_sum(tiles):
    sm = tiles[0]
    for t in tiles[1:]:
        sm = sm + t
    return jnp.sum(sm, axis=1, keepdims=True)


def _nsa_cmp_body(q_ref, kc_ref, vc_ref, zg_ref, gl_ref, wov_ref,
                  oc_ref, sel_ref, any_ref, o_scr, imp_scr, *, tq, n_sel):
    g = pl.program_id(1)
    t0 = pl.program_id(2) * tq
    ncp = kc_ref.shape[2]
    nsp = wov_ref.shape[0]

    def attend(width):
        qs = _stack_heads(q_ref[...])
        slopes = _head_slopes(g)
        n = lax.broadcasted_iota(jnp.int32, (1, width), 1)
        rel = ((n * CMP_STRIDE - t0).astype(F32) + 0.5 * (CMP_BLK - 1))
        c_end = lax.broadcasted_iota(jnp.int32, (tq, width), 1) * CMP_STRIDE + (CMP_BLK - 1)
        tok = t0 + lax.broadcasted_iota(jnp.int32, (tq, width), 0)
        madd = jnp.where(c_end <= tok, 0.0, NEG)
        s = _dot_nt(qs, kc_ref[0, 0, 0:width, :])
        p_rows = []
        psum_tiles = None
        for h in range(B_HPG):
            tiles = _head_logits(s, h, tq, slopes[h], rel, madd)
            m = jnp.maximum(_tile_max(tiles), M_INIT)
            es = [jnp.exp2(t - m) for t in tiles]
            inv = 1.0 / jnp.maximum(_tile_sum(es), 1e-30)
            ps = [e * inv for e in es]
            psum_tiles = ps if psum_tiles is None else [a + b_ for a, b_ in zip(psum_tiles, ps)]
            p_rows.append(jnp.concatenate([p.astype(BF16) for p in ps], axis=1))
        o_scr[...] = _dot(jnp.concatenate(p_rows, axis=0), vc_ref[0, 0, 0:width, :])
        psum = jnp.concatenate(psum_tiles, axis=1)
        p_hi = psum.astype(BF16)
        p_lo = (psum - p_hi.astype(F32)).astype(BF16)
        wov = wov_ref[:, 0:width]
        imp_scr[...] = _dot_nt(wov, p_hi) + _dot_nt(wov, p_lo)

    ntile = ncp // LANES
    need = _shr(t0 + tq - 1, CMP_STRIDE * LANES) + 1
    for k in range(1, ntile + 1):
        @pl.when((need == k) if k < ntile else (need >= k))
        def _(k=k):
            attend(k * LANES)

    o = o_scr[...]
    imp = imp_scr[...]
    j = lax.broadcasted_iota(jnp.int32, (nsp, tq), 0)
    cur = _shr(t0 + lax.broadcasted_iota(jnp.int32, (1, tq), 1), SEL_BLK)
    forced = (j == 0) | (j == cur) | (j == cur - 1)
    val = jnp.where(forced, jnp.inf, jnp.where(j <= cur, imp, -jnp.inf))
    sel_t = jnp.zeros((nsp, tq), F32)
    for _ in range(n_sel):
        mx = jnp.max(val, axis=0, keepdims=True)
        cand = jnp.where((val == mx) & (mx > -jnp.inf), j, nsp)
        pick = j == jnp.min(cand, axis=0, keepdims=True)
        sel_t = jnp.where(pick, 1.0, sel_t)
        val = jnp.where(pick, -jnp.inf, val)
    sel = sel_t.T
    sel_ref[0, 0] = sel.astype(sel_ref.dtype)
    any_ref[0, 0, 0] = jnp.broadcast_to(jnp.max(sel, axis=0, keepdims=True), (SUBLANES, nsp))
    gate = gl_ref[...]
    for h in range(B_HPG):
        hs = slice(h * B_DK, (h + 1) * B_DK)
        oc_ref[:, hs] = gate[:, h:h + 1] * zg_ref[:, hs] * o[h * tq:(h + 1) * tq]


def _nsa_cmp(q, kvc, zg, gates, wov_t, batch, seq, n_sel):
    t = q.shape[0]
    tq = Q_TILE
    nqb = seq // tq
    ncp = kvc.shape[2]
    nsp = wov_t.shape[0]
    gw = B_HPG * B_DK
    return pl.pallas_call(
        functools.partial(_nsa_cmp_body, tq=tq, n_sel=n_sel),
        grid=(batch, B_GROUPS, nqb),
        in_specs=[pl.BlockSpec((tq, gw), lambda b, g, i: (b * nqb + i, g)),
                  pl.BlockSpec((1, 1, ncp, B_DK), lambda b, g, i: (g, b, 0, 0)),
                  pl.BlockSpec((1, 1, ncp, B_DK), lambda b, g, i: (B_GROUPS + g, b, 0, 0)),
                  pl.BlockSpec((tq, gw), lambda b, g, i: (b * nqb + i, g)),
                  pl.BlockSpec((tq, LANES), lambda b, g, i: (b * nqb + i, g)),
                  pl.BlockSpec((nsp, ncp), lambda b, g, i: (0, 0))],
        out_specs=[pl.BlockSpec((tq, gw), lambda b, g, i: (b * nqb + i, g)),
                   pl.BlockSpec((1, 1, tq, nsp), lambda b, g, i: (b, g, i, 0)),
                   pl.BlockSpec((1, 1, 1, SUBLANES, nsp), lambda b, g, i: (b, g, i, 0, 0))],
        out_shape=[jax.ShapeDtypeStruct((t, B_GROUPS * gw), F32),
                   jax.ShapeDtypeStruct((batch, B_GROUPS, seq, nsp), BF16),
                   jax.ShapeDtypeStruct((batch, B_GROUPS, nqb, SUBLANES, nsp), F32)],
        scratch_shapes=[pltpu.VMEM((B_HPG * tq, B_DK), F32), pltpu.VMEM((nsp, tq), F32)],
        compiler_params=_params(3),
        name="nsa_cmp",
    )(q, kvc, kvc, zg, gates, wov_t)


def _softmax_once(s, v, madd, tq):
    ms = [jnp.maximum(_tile_max(_head_logits(s, h, tq, None, None, madd)), M_INIT)
          for h in range(B_HPG)]
    p_rows = []
    for h in range(B_HPG):
        tiles = _head_logits(s, h, tq, None, None, madd)
        p_rows.append(jnp.concatenate([jnp.exp2(t - ms[h]).astype(BF16) for t in tiles], axis=1))
    pv = _dot(jnp.concatenate(p_rows, axis=0), _with_ones(v))
    return pv[:, :B_DK] * (1.0 / jnp.maximum(pv[:, B_DK:], 1e-30))


def _online_update(s, v_c, slopes, rel, madd, m_ref, l_ref, acc_ref, tq):
    slope_of = (lambda h: None) if slopes is None else (lambda h: slopes[h])
    p_rows = []
    alphas = []
    m_news = []
    for h in range(B_HPG):
        rows = slice(h * tq, (h + 1) * tq)
        m_prev = m_ref[rows, :]
        m_new = jnp.maximum(m_prev, _tile_max(_head_logits(s, h, tq, slope_of(h), rel, madd)))
        m_ref[rows, :] = m_new
        alphas.append(jnp.exp2(m_prev - m_new))
        m_news.append(m_new)
    for h in range(B_HPG):
        tiles = _head_logits(s, h, tq, slope_of(h), rel, madd)
        p_rows.append(jnp.concatenate([jnp.exp2(t - m_news[h]).astype(BF16) for t in tiles], axis=1))
    pv = _dot(jnp.concatenate(p_rows, axis=0), _with_ones(v_c))
    for h in range(B_HPG):
        rows = slice(h * tq, (h + 1) * tq)
        acc_ref[rows, :] = alphas[h] * acc_ref[rows, :] + pv[rows, :B_DK]
        l_ref[rows, :] = alphas[h] * l_ref[rows, :] + pv[rows, B_DK:]


def _softmax_reset(m_ref, l_ref, acc_ref):
    m_ref[...] = jnp.full(m_ref.shape, M_INIT, F32)
    l_ref[...] = jnp.zeros(l_ref.shape, F32)
    acc_ref[...] = jnp.zeros(acc_ref.shape, F32)


def _softmax_result(l_ref, acc_ref):
    return acc_ref[...] * (1.0 / jnp.maximum(l_ref[...], 1e-30))


def _nsa_sw_body(lst_ref, q_ref, sel_ref, ks_ref, vs_ref, kw_ref, vw_ref,
                 zs_ref, zw_ref, gl_ref, oc_ref, out_ref,
                 qs_ref, s_a, s_b, sw_ref, m_ref, l_ref, acc_ref,
                 *, tq, nqb, nch):
    b = pl.program_id(0)
    g = pl.program_id(1)
    i = pl.program_id(2)
    t0 = i * tq
    qs_ref[:, :B_DK] = _stack_heads(q_ref[...])
    not_sel = (1.0 - sel_ref[0, 0].astype(F32)).astype(BF16)
    qs_ref[:, B_DK:] = jnp.concatenate([not_sel] * B_HPG, axis=0)
    slopes = _head_slopes(g)
    nsp = sel_ref.shape[3]
    ch = SEL_CHUNK
    bpc = ch // SEL_BLK
    lbase = ((b * B_GROUPS + g) * nqb + i) * (nch + 2)
    n_reg = lst_ref[lbase] - 1

    def positions(start, width):
        rel = (start - t0) + lax.broadcasted_iota(jnp.int32, (1, width), 1)
        dist = lax.broadcasted_iota(jnp.int32, (tq, width), 0) - (
            (start - t0) + lax.broadcasted_iota(jnp.int32, (tq, width), 1))
        return rel.astype(F32), dist

    def scores(c):
        start = pl.multiple_of(c * ch, ch)
        key_blk = c * bpc + _shr(lax.broadcasted_iota(jnp.int32, (ch, nsp), 0), SEL_BLK)
        own_blk = jnp.where(lax.broadcasted_iota(jnp.int32, (ch, nsp), 1) == key_blk, MASK_NEG, 0.0)
        keys = jnp.concatenate([ks_ref[0, 0, pl.ds(start, ch), :], own_blk.astype(BF16)], axis=1)
        return _dot_nt(qs_ref[...], keys)

    def stage(j, s_ref):
        c = lst_ref[lbase + 1 + j]
        s = scores(c)
        rel, _ = positions(c * ch, ch)
        for h in range(B_HPG):
            for jt, t in enumerate(_head_logits(s, h, tq, slopes[h], rel, None)):
                s_ref[h * tq:(h + 1) * tq, jt * LANES:(jt + 1) * LANES] = t

    def consume(j, s_ref):
        start = pl.multiple_of(lst_ref[lbase + 1 + j] * ch, ch)
        _online_update(s_ref, vs_ref[0, 0, pl.ds(start, ch), :], None, None, None,
                       m_ref, l_ref, acc_ref, tq)

    nwin = WINDOW // ch + max(tq // ch, 1)
    c_lo = jnp.maximum(_shr(t0 + tq - 1, ch) - (nwin - 1), 0)
    st = pl.multiple_of(c_lo * ch, ch)

    def consume_own(s_ref):
        sw = _dot_nt(qs_ref[:, :B_DK], kw_ref[0, 0, pl.ds(st, nwin * ch), :])
        rel_w, _ = positions(st, nwin * ch)
        for h in range(B_HPG):
            for jt, t in enumerate(_head_logits(sw, h, tq, slopes[h], rel_w, None)):
                sw_ref[h * tq:(h + 1) * tq, jt * LANES:(jt + 1) * LANES] = t
        start = pl.multiple_of(lst_ref[lbase + 1 + n_reg] * ch, ch)
        _, dist = positions(start, ch)
        _online_update(s_ref, vs_ref[0, 0, pl.ds(start, ch), :], None, None,
                       jnp.where(dist >= 0, 0.0, NEG), m_ref, l_ref, acc_ref, tq)

    _softmax_reset(m_ref, l_ref, acc_ref)
    stage(0, s_a)

    def chunk_pair(jj, carry):
        j = 2 * jj
        stage(j + 1, s_b)
        consume(j, s_a)

        @pl.when(j + 1 < n_reg)
        def _():
            stage(j + 2, s_a)
            consume(j + 1, s_b)
        return carry

    lax.fori_loop(0, _shr(n_reg + 1, 2), chunk_pair, 0)

    @pl.when((n_reg & 1) == 0)
    def _():
        consume_own(s_a)

    @pl.when((n_reg & 1) == 1)
    def _():
        consume_own(s_b)

    o_s = _softmax_result(l_ref, acc_ref)

    _, dist = positions(st, nwin * ch)
    madd = jnp.where(dist >= 0, jnp.where(dist < WINDOW, 0.0, NEG), NEG)
    o_w = _softmax_once(sw_ref, vw_ref[0, 0, pl.ds(st, nwin * ch), :], madd, tq)

    gate = gl_ref[...]
    for h in range(B_HPG):
        hs = slice(h * B_DK, (h + 1) * B_DK)
        rs = slice(h * tq, (h + 1) * tq)
        merged = (oc_ref[:, hs]
                  + gate[:, B_HPG + h:B_HPG + h + 1] * zs_ref[:, hs] * o_s[rs]
                  + gate[:, 2 * B_HPG + h:2 * B_HPG + h + 1] * zw_ref[:, hs] * o_w[rs])
        out_ref[:, hs] = merged.astype(out_ref.dtype)


def _nsa_sw(flags, q, sel, kv_att, zg, gates, oc, batch, seq):
    t = q.shape[0]
    tq = Q_TILE
    nqb = seq // tq
    nch = seq // SEL_CHUNK
    nsp = sel.shape[3]
    gw = B_HPG * B_DK
    rows = B_HPG * tq
    tok = lambda j: pl.BlockSpec((tq, gw), lambda b, g, i, f, j=j: (b * nqb + i, j * B_GROUPS + g))
    res = lambda k: pl.BlockSpec((1, 1, seq, B_DK), lambda b, g, i, f, k=k: (k * B_GROUPS + g, b, 0, 0))
    grid_spec = pltpu.PrefetchScalarGridSpec(
        num_scalar_prefetch=1,
        grid=(batch, B_GROUPS, nqb),
        in_specs=[tok(0),
                  pl.BlockSpec((1, 1, tq, nsp), lambda b, g, i, f: (b, g, i, 0)),
                  res(0), res(1), res(2), res(3),
                  tok(1), tok(2),
                  pl.BlockSpec((tq, LANES), lambda b, g, i, f: (b * nqb + i, g)),
                  tok(0)],
        out_specs=tok(0),
        scratch_shapes=[pltpu.VMEM((rows, B_DK + nsp), BF16),
                        pltpu.VMEM((rows, SEL_CHUNK), F32),
                        pltpu.VMEM((rows, SEL_CHUNK), F32),
                        pltpu.VMEM((rows, WINDOW + max(tq, SEL_CHUNK)), F32),
                        pltpu.VMEM((rows, LANES), F32),
                        pltpu.VMEM((rows, LANES), F32),
                        pltpu.VMEM((rows, B_DK), F32)])
    return pl.pallas_call(
        functools.partial(_nsa_sw_body, tq=tq, nqb=nqb, nch=nch),
        grid_spec=grid_spec,
        out_shape=jax.ShapeDtypeStruct((t, B_GROUPS * gw), BF16),
        compiler_params=_params(3),
        name="nsa_sw",
    )(flags, q, sel, kv_att, kv_att, kv_att, kv_att, zg, zg, gates, oc)


def _blockdiag_dense(w):
    nb, bs, _ = w.shape
    per = MXU_DIM // bs
    w4 = w.reshape(nb // per, per, bs, bs)
    dense = jnp.einsum("sgij,gh->sgihj", w4, jnp.eye(per, dtype=w.dtype))
    return dense.reshape(nb // per, MXU_DIM, MXU_DIM).astype(BF16)


def _pad_cols(w, width):
    return jnp.pad(w, ((0, 0), (0, width - w.shape[1])))


def _overlap_matrix_t(nr, nsp, nc, ns):
    i = jnp.arange(nr)[None, :] * CMP_STRIDE
    j = jnp.arange(nsp)[:, None] * SEL_BLK
    ov = jnp.minimum(i + CMP_BLK, j + SEL_BLK) - jnp.maximum(i, j)
    ov = jnp.maximum(ov, 0) / CMP_STRIDE
    live = (jnp.arange(nr)[None, :] < nc) & (jnp.arange(nsp)[:, None] < ns)
    return jnp.where(live, ov, 0).astype(BF16)


def _mlstm_layer(xf, batch, seq, g_pre, g_post, w_in, conv_w, conv_b, w_q, w_k, w_v,
                 w_gate, b_gate, head_norm, skip, w_out):
    c = conv_w.shape[1]
    dh = c // A_HEADS
    proj = _rms_matmul(xf, g_pre, w_in.astype(BF16), F32)
    wg = [_pad_cols(w_gate[j * c:(j + 1) * c], LANES).astype(BF16) for j in range(3)]
    bg = _pad_cols(b_gate.reshape(1, -1), LANES).astype(F32)
    q, k, v, xc, gates = _a_local(
        proj, seq, conv_w.astype(F32), conv_b.reshape(1, c).astype(F32),
        _blockdiag_dense(w_q), _blockdiag_dense(w_k), _blockdiag_dense(w_v),
        wg[0], wg[1], wg[2], bg, float(dh) ** -0.5)
    y = _a_cell(q, k, v, xc, proj, gates, head_norm.reshape(1, c).astype(F32),
                skip.reshape(1, c).astype(F32), batch, seq)
    return _out_proj(y, w_out.astype(BF16), g_post, xf)


def _shared_kv(xf, batch, seq, kv_norm, kv_w, cmp_pos, cmp_w1, cmp_b1, cmp_w2, cmp_b2):
    ncmp = 2 * B_GROUPS * B_DK
    kv_w = kv_w.astype(BF16)
    kv_cmp, h = _rms_matmul(xf, kv_norm, kv_w[:, :ncmp], F32, split=True, emit_h=True)
    kv_att = _matmul(h, kv_w[:, ncmp:], BF16, split=True)
    nr = seq // CMP_STRIDE
    half = CMP_STRIDE * B_DK
    pos_a = cmp_pos[:, :CMP_STRIDE].reshape(2, 1, half).astype(F32)
    pos_b = cmp_pos[:, CMP_STRIDE:].reshape(2, 1, half).astype(F32)
    kvc = _kv_compress(kv_cmp.reshape(2 * B_GROUPS, batch, nr, half), pos_a, pos_b,
                       cmp_w1[:, :half].astype(BF16), cmp_w1[:, half:].astype(BF16),
                       cmp_b1.reshape(2, 1, -1).astype(F32), cmp_w2.astype(BF16),
                       cmp_b2.reshape(2, 1, -1).astype(F32))
    return kvc, kv_att.reshape(4 * B_GROUPS, batch, seq, B_DK)


def _nsa_layer(xf, batch, seq, g_pre, g_post, w_in, w_out, shared, wov_t, n_sel):
    kvc, kv_att = shared
    nqk = B_HEADS * B_DK
    ngl = N_BRANCH * B_HEADS
    w_q = w_in[:, :nqk].astype(BF16)
    w_gl = w_in[:, nqk:nqk + ngl].reshape(-1, N_BRANCH, B_GROUPS, B_HPG)
    w_gl = jnp.transpose(w_gl, (0, 2, 1, 3)).reshape(-1, B_GROUPS, N_BRANCH * B_HPG)
    w_gl = jnp.pad(w_gl, ((0, 0), (0, 0), (0, LANES - N_BRANCH * B_HPG))).reshape(-1, B_GROUPS * LANES)
    q, h = _rms_matmul(xf, g_pre, w_q, BF16, scale=LOG2E * float(B_DK) ** -0.5, emit_h=True)
    zg = _matmul(h, w_in[:, nqk + ngl:].astype(BF16), F32, act="silu")
    gates = _matmul(h, w_gl.astype(BF16), F32, act="sigmoid")
    oc, sel, blk_any = _nsa_cmp(q, kvc, zg, gates, wov_t, batch, seq, n_sel)
    ns = seq // SEL_BLK
    bpc = SEL_CHUNK // SEL_BLK
    flags = blk_any[:, :, :, 0, :ns].reshape(batch, B_GROUPS, seq // Q_TILE, ns // bpc, bpc)
    flags = (jnp.max(flags, axis=-1) > 0).astype(jnp.int32)
    order = jnp.argsort(1 - flags, axis=-1, stable=True).astype(jnp.int32)
    count = jnp.sum(flags, axis=-1, keepdims=True)
    lists = jnp.concatenate([count, order, jnp.zeros_like(count)], axis=-1).reshape(-1)
    out = _nsa_sw(lists, q, sel, kv_att, zg, gates, oc, batch, seq)
    return _out_proj(out, w_out.astype(BF16), g_post, xf)


def kernel(x, norm_pre, norm_post, a_w_in, a_conv_w, a_conv_b, a_w_q, a_w_k, a_w_v, a_w_gate,
           a_b_gate, a_head_norm, a_skip, a_w_out, kv_norm, kv_w, cmp_pos, cmp_w1, cmp_b1,
           cmp_w2, cmp_b2, b_w_in, b_w_out):
    batch, seq, d = x.shape
    na = a_w_in.shape[0]
    nb = b_w_in.shape[0]
    assert seq % (CMP_STRIDE * LANES) == 0 and seq % SEL_CHUNK == 0 and seq % A_CHUNK == 0
    xf = x.reshape(batch * seq, d).astype(F32)
    for l in range(na):
        xf = _mlstm_layer(xf, batch, seq, norm_pre[l], norm_post[l], a_w_in[l], a_conv_w[l],
                          a_conv_b[l], a_w_q[l], a_w_k[l], a_w_v[l], a_w_gate[l], a_b_gate[l],
                          a_head_norm[l], a_skip[l], a_w_out[l])
    shared = _shared_kv(xf, batch, seq, kv_norm, kv_w, cmp_pos, cmp_w1, cmp_b1, cmp_w2, cmp_b2)
    nr = seq // CMP_STRIDE
    ns = seq // SEL_BLK
    nsp = -(-ns // LANES) * LANES
    wov_t = _overlap_matrix_t(nr, nsp, nr - CMP_BLK // CMP_STRIDE + 1, ns)
    for l in range(nb):
        xf = _nsa_layer(xf, batch, seq, norm_pre[na + l], norm_post[na + l], b_w_in[l],
                        b_w_out[l], shared, wov_t, min(SEL_TOP, ns))
    return xf.reshape(batch, seq, d).astype(x.dtype)
```

```python
import functools

import jax
import jax.numpy as jnp
from jax import lax
from jax.experimental import pallas as pl
from jax.experimental.pallas import tpu as pltpu

F32 = jnp.float32
BF16 = jnp.bfloat16
EPS = 1e-6

A_HEADS = 4
A_CONV = 4
B_HEADS = 16
B_GROUPS = 2
B_HPG = B_HEADS // B_GROUPS
B_DK = 128
N_BRANCH = 3
CMP_BLK = 32
CMP_STRIDE = 16
CMP_HIDDEN = 256
SEL_BLK = 64
SEL_TOP = 16
WINDOW = 512

LANES = 128
SUBLANES = 8
MXU_DIM = 256
VMEM_LIMIT_BYTES = 56 * 1024 * 1024

A_CHUNK = 256
A_LOCAL_TILE = 256
Q_TILE = 256
SEL_CHUNK = 256
NEG = -1e30
M_INIT = -1e20
MASK_NEG = -2.0 ** 100
assert B_DK == LANES


def _params(n_axes):
    return pltpu.CompilerParams(
        dimension_semantics=("arbitrary",) * n_axes,
        vmem_limit_bytes=VMEM_LIMIT_BYTES)


def _sigmoid(v):
    return 1.0 / (1.0 + jnp.exp(-v))


def _silu(v):
    return v * _sigmoid(v)


def _shr(v, pow2):
    assert pow2 & (pow2 - 1) == 0
    return lax.shift_right_logical(v, jnp.int32(pow2.bit_length() - 1))


def _dot(a, b):
    return jnp.dot(a, b, preferred_element_type=F32)


def _dot_nt(a, b):
    return lax.dot_general(a, b, (((1,), (1,)), ((), ())), preferred_element_type=F32)


def _dot_tn(a, b):
    return lax.dot_general(a, b, (((0,), (0,)), ((), ())), preferred_element_type=F32)


_ACTS = {None: lambda v: v, "sigmoid": _sigmoid, "silu": _silu}


def _store_proj(o_ref, acc, scale, acts):
    if scale != 1.0:
        acc = acc * scale

    def store(act):
        val = _ACTS[act](acc)
        if len(o_ref.shape) == 2:
            o_ref[...] = val.astype(o_ref.dtype)
        else:
            for j in range(o_ref.shape[0]):
                o_ref[j] = val[:, j * LANES:(j + 1) * LANES].astype(o_ref.dtype)

    if len(acts) == 1:
        store(acts[0])
    else:
        for idx, act in enumerate(acts):
            pl.when(pl.program_id(0) == idx)(functools.partial(store, act))


def _rms_matmul_body(x_ref, g_ref, w_ref, o_ref, *h_ref, scale, acts):
    x = x_ref[...]
    ms = jnp.mean(x * x, axis=-1, keepdims=True)
    h = ((x * lax.rsqrt(ms + EPS)) * g_ref[...]).astype(BF16)
    if h_ref:
        h_ref[0][...] = h
    _store_proj(o_ref, _dot(h, w_ref[...]), scale, acts)


def _matmul_body(h_ref, w_ref, o_ref, *, acts):
    _store_proj(o_ref, _dot(h_ref[...], w_ref[...]), 1.0, acts)


def _pick_tile(n, candidates):
    for c in candidates:
        if n % c == 0:
            return c
    return n


def _proj_tiles(t, n, split, out_dtype):
    tm = _pick_tile(t, (1024, 512, 256, 128))
    tn = _pick_tile(n, (2048, 1536, 1280, 1024, 896, 768, 512, 256, 128))
    if split:
        out_spec = pl.BlockSpec((tn // LANES, tm, LANES), lambda j, i: (j, i, 0))
        out_shape = jax.ShapeDtypeStruct((n // LANES, t, LANES), out_dtype)
    else:
        out_spec = pl.BlockSpec((tm, tn), lambda j, i: (i, j))
        out_shape = jax.ShapeDtypeStruct((t, n), out_dtype)
    return tm, tn, out_spec, out_shape


def _rms_matmul(x, g, w, out_dtype, scale=1.0, split=False, emit_h=False, acts=(None,)):
    t, d = x.shape
    n = w.shape[1]
    tm, tn, out_spec, out_shape = _proj_tiles(t, n, split, out_dtype)
    assert len(acts) in (1, n // tn)
    out_specs, out_shapes = [out_spec], [out_shape]
    if emit_h:
        assert n == tn
        out_specs.append(pl.BlockSpec((tm, d), lambda j, i: (i, 0)))
        out_shapes.append(jax.ShapeDtypeStruct((t, d), BF16))
    res = pl.pallas_call(
        functools.partial(_rms_matmul_body, scale=scale, acts=acts),
        grid=(n // tn, t // tm),
        in_specs=[pl.BlockSpec((tm, d), lambda j, i: (i, 0)),
                  pl.BlockSpec((1, d), lambda j, i: (0, 0)),
                  pl.BlockSpec((d, tn), lambda j, i: (0, j))],
        out_specs=out_specs,
        out_shape=out_shapes,
        compiler_params=_params(2),
        name="rms_matmul",
    )(x, g.reshape(1, d).astype(F32), w)
    return res if emit_h else res[0]


def _matmul(h, w, out_dtype, split=False, act=None):
    t, d = h.shape
    n = w.shape[1]
    tm, tn, out_spec, out_shape = _proj_tiles(t, n, split, out_dtype)
    return pl.pallas_call(
        functools.partial(_matmul_body, acts=(act,)),
        grid=(n // tn, t // tm),
        in_specs=[pl.BlockSpec((tm, d), lambda j, i: (i, 0)),
                  pl.BlockSpec((d, tn), lambda j, i: (0, j))],
        out_specs=out_spec,
        out_shape=out_shape,
        compiler_params=_params(2),
        name="matmul",
    )(h, w)


def _out_proj_body(y_ref, w_ref, g_ref, x_ref, o_ref):
    r = _dot(y_ref[...], w_ref[...])
    ms = jnp.mean(r * r, axis=-1, keepdims=True)
    o_ref[...] = x_ref[...] + (r * lax.rsqrt(ms + EPS)) * g_ref[...]


def _out_proj(y, w, g, x):
    t, k = y.shape
    d = w.shape[1]
    tm = _pick_tile(t, (1024, 512, 256, 128))
    return pl.pallas_call(
        _out_proj_body,
        grid=(t // tm,),
        in_specs=[pl.BlockSpec((tm, k), lambda i: (i, 0)),
                  pl.BlockSpec((k, d), lambda i: (0, 0)),
                  pl.BlockSpec((1, d), lambda i: (0, 0)),
                  pl.BlockSpec((tm, d), lambda i: (i, 0))],
        out_specs=pl.BlockSpec((tm, d), lambda i: (i, 0)),
        out_shape=jax.ShapeDtypeStruct((t, d), F32),
        compiler_params=_params(1),
        name="out_proj",
    )(y, w, g.reshape(1, d).astype(F32), x)


HALO = SUBLANES


def _a_local_body(xm_ref, cw_ref, cb_ref, bdq_ref, bdk_ref, bdv_ref,
                  wgq_ref, wgk_ref, wgv_ref, bg_ref,
                  q_ref, k_ref, v_ref, xc_ref, gates_ref, ext_ref, cv_ref,
                  *, tl, tiles_per_seq, kscale):
    i = pl.program_id(0)
    c = xm_ref.shape[1]
    first = lax.rem(i, tiles_per_seq) == 0

    @pl.when(first)
    def _():
        ext_ref[0:HALO, :] = jnp.zeros((HALO, c), F32)

    @pl.when(jnp.logical_not(first))
    def _():
        ext_ref[0:HALO, :] = ext_ref[tl:tl + HALO, :]

    xm = xm_ref[...]
    ext_ref[HALO:HALO + tl, :] = xm
    acc = jnp.broadcast_to(cb_ref[...], (tl, c))
    for j in range(A_CONV):
        off = HALO - (A_CONV - 1) + j
        acc = acc + cw_ref[j:j + 1, :] * ext_ref[off:off + tl, :]
    cv_ref[...] = acc
    xc = _silu(cv_ref[...])
    xcb = xc.astype(BF16)
    xmb = xm.astype(BF16)
    xc_ref[...] = xcb
    gates = jnp.broadcast_to(bg_ref[...], (tl, LANES))
    for s in range(c // MXU_DIM):
        sl = slice(MXU_DIM * s, MXU_DIM * (s + 1))
        qb = _dot(xcb[:, sl], bdq_ref[s]).astype(BF16)
        kf = _dot(xcb[:, sl], bdk_ref[s])
        kb = kf.astype(BF16)
        vb = _dot(xmb[:, sl], bdv_ref[s]).astype(BF16)
        gates = gates + _dot(qb, wgq_ref[sl, :]) + _dot(kb, wgk_ref[sl, :]) + _dot(vb, wgv_ref[sl, :])
        q_ref[:, sl] = qb
        k_ref[:, sl] = (kf * kscale).astype(BF16)
        v_ref[:, sl] = vb
    gates_ref[...] = gates


def _a_local(xm_src, seq, cw, cb, bdq, bdk, bdv, wgq, wgk, wgv, bg, kscale):
    t = xm_src.shape[0]
    c = cw.shape[1]
    tl = A_LOCAL_TILE
    nslab = c // MXU_DIM
    full = lambda shape: pl.BlockSpec(shape, lambda i: (0,) * len(shape))
    tok = lambda w: pl.BlockSpec((tl, w), lambda i: (i, 0))
    return pl.pallas_call(
        functools.partial(_a_local_body, tl=tl, tiles_per_seq=seq // tl, kscale=kscale),
        grid=(t // tl,),
        in_specs=[tok(c), full((A_CONV, c)), full((1, c)),
                  full((nslab, MXU_DIM, MXU_DIM)), full((nslab, MXU_DIM, MXU_DIM)),
                  full((nslab, MXU_DIM, MXU_DIM)),
                  full((c, LANES)), full((c, LANES)), full((c, LANES)), full((1, LANES))],
        out_specs=[tok(c), tok(c), tok(c), tok(c), tok(LANES)],
        out_shape=[jax.ShapeDtypeStruct((t, c), BF16)] * 4 + [jax.ShapeDtypeStruct((t, LANES), F32)],
        scratch_shapes=[pltpu.VMEM((tl + 2 * HALO, c), F32), pltpu.VMEM((tl, c), F32)],
        compiler_params=_params(1),
        name="a_local",
    )(xm_src, cw, cb, bdq, bdk, bdv, wgq, wgk, wgv, bg)


def _a_cell_body(q_ref, k_ref, v_ref, xc_ref, o_ref, z_ref, gates_ref, hn_ref, skip_ref,
                 y_ref, c_ref, m_ref, *, chunk, dh):
    cidx = pl.program_id(1)
    nh = A_HEADS

    @pl.when(cidx == 0)
    def _():
        c_ref[...] = jnp.zeros(c_ref.shape, F32)
        m_ref[...] = jnp.zeros(m_ref.shape, F32)

    gates = gates_ref[...]
    lf = jnp.minimum(gates, 0.0) - jnp.log(1.0 + jnp.exp(-jnp.abs(gates)))
    r_i = lax.broadcasted_iota(jnp.int32, (chunk, chunk), 0)
    c_i = lax.broadcasted_iota(jnp.int32, (chunk, chunk), 1)
    causal = r_i >= c_i
    tril = causal.astype(BF16)
    hi = lf.astype(BF16)
    r1 = lf - hi.astype(F32)
    mid = r1.astype(BF16)
    lo = (r1 - mid.astype(F32)).astype(BF16)
    b_col = _dot(tril, hi) + _dot(tril, mid) + _dot(tril, lo)
    g_t = gates.T
    b_t = b_col.T

    for h in range(nh):
        hs = slice(h * dh, (h + 1) * dh)
        qh = q_ref[:, hs]
        kh = k_ref[:, hs]
        vh = v_ref[:, hs]
        i_row = g_t[h:h + 1, :]
        b_row = b_t[nh + h:nh + h + 1, :]
        i_c = gates[:, h:h + 1]
        b_c = b_col[:, nh + h:nh + h + 1]
        m_b = m_ref[h:h + 1, 0:1]
        log_d = jnp.where(causal, b_c - b_row + i_row, -jnp.inf)
        m_inter = b_c + m_b
        m_t = jnp.maximum(jnp.max(log_d, axis=1, keepdims=True), m_inter)
        d = jnp.exp(log_d - m_t)
        s = _dot_nt(qh, kh) * d
        decay = jnp.exp(m_inter - m_t)
        inter = _dot(qh, c_ref[h].astype(BF16))
        intra = _dot(s.astype(BF16), vh)
        num = decay * inter[:, :dh] + intra
        den = decay * inter[:, dh:] + jnp.sum(s, axis=1, keepdims=True)
        scale = 1.0 / jnp.maximum(jnp.abs(den), jnp.exp(-m_t))
        b_last = b_col[chunk - 1:chunk, nh + h:nh + h + 1]
        m_new = jnp.maximum(b_last + m_b,
                            jnp.max(b_last - b_row + i_row, axis=1, keepdims=True))
        w_c = jnp.exp(b_last - b_c + i_c - m_new)
        carry = jnp.exp(b_last + m_b - m_new)
        wv = jnp.concatenate([vh.astype(F32) * w_c, jnp.broadcast_to(w_c, (chunk, LANES))],
                             axis=1).astype(BF16)
        c_ref[h] = carry * c_ref[h] + _dot_tn(kh, wv)
        m_ref[h:h + 1, :] = jnp.broadcast_to(m_new, (1, LANES))
        lanes = [slice(h * dh + j * LANES, h * dh + (j + 1) * LANES) for j in range(dh // LANES)]
        hc = [_sigmoid(o_ref[:, ls]) * (num[:, j * LANES:(j + 1) * LANES] * scale)
              for j, ls in enumerate(lanes)]
        mu = _tile_sum(hc) * (1.0 / dh)
        cen = [t - mu for t in hc]
        rstd = lax.rsqrt(_tile_sum([t * t for t in cen]) * (1.0 / dh) + EPS)
        for t, ls in zip(cen, lanes):
            y = (t * rstd * hn_ref[:, ls] + skip_ref[:, ls] * xc_ref[:, ls].astype(F32)) * _silu(z_ref[:, ls])
            y_ref[:, ls] = y.astype(BF16)


def _a_cell(q, k, v, xc, proj, gates, head_norm, skip, batch, seq):
    t, c = q.shape
    chunk = A_CHUNK
    nc = seq // chunk
    dh = c // A_HEADS
    tokc = lambda j: pl.BlockSpec((chunk, c), lambda b, i, j=j: (b * nc + i, j))
    full = lambda shape: pl.BlockSpec(shape, lambda b, i: (0,) * len(shape))
    return pl.pallas_call(
        functools.partial(_a_cell_body, chunk=chunk, dh=dh),
        grid=(batch, nc),
        in_specs=[tokc(0), tokc(0), tokc(0), tokc(0), tokc(1), tokc(2),
                  pl.BlockSpec((chunk, LANES), lambda b, i: (b * nc + i, 0)),
                  full((1, c)), full((1, c))],
        out_specs=tokc(0),
        out_shape=jax.ShapeDtypeStruct((t, c), BF16),
        scratch_shapes=[pltpu.VMEM((A_HEADS, dh, dh + LANES), F32),
                        pltpu.VMEM((SUBLANES, LANES), F32)],
        compiler_params=_params(2),
        name="a_cell",
    )(q, k, v, xc, proj, proj, gates, head_norm, skip)


def _kv_compress_body(x_ref, pa_ref, pb_ref, w1a_ref, w1b_ref, b1_ref, w2_ref, b2_ref,
                      o_ref, nxt_ref):
    x = x_ref[0, 0]
    nr = x.shape[0]
    first = _dot((x + pa_ref[0]).astype(BF16), w1a_ref[0])
    nxt_ref[0:nr, :] = _dot((x + pb_ref[0]).astype(BF16), w1b_ref[0])
    nxt_ref[nr:nr + SUBLANES, :] = jnp.zeros((SUBLANES, nxt_ref.shape[1]), F32)
    h1 = _silu(first + nxt_ref[1:nr + 1, :] + b1_ref[0])
    o_ref[0, 0] = (_dot(h1.astype(BF16), w2_ref[0]) + b2_ref[0]).astype(o_ref.dtype)


def _kv_compress(x16, pos_a, pos_b, w1a, w1b, b1, w2, b2):
    nkg, batch, nr, half = x16.shape
    hid = w1a.shape[2]
    dk = w2.shape[2]
    kind = lambda shape: pl.BlockSpec((1,) + shape, lambda p, b: (p // B_GROUPS, 0, 0))
    return pl.pallas_call(
        _kv_compress_body,
        grid=(nkg, batch),
        in_specs=[pl.BlockSpec((1, 1, nr, half), lambda p, b: (p, b, 0, 0)),
                  kind((1, half)), kind((1, half)),
                  kind((half, hid)), kind((half, hid)), kind((1, hid)),
                  kind((hid, dk)), kind((1, dk))],
        out_specs=pl.BlockSpec((1, 1, nr, dk), lambda p, b: (p, b, 0, 0)),
        out_shape=jax.ShapeDtypeStruct((nkg, batch, nr, dk), BF16),
        scratch_shapes=[pltpu.VMEM((nr + SUBLANES, hid), F32)],
        compiler_params=_params(2),
        name="kv_compress",
    )(x16, pos_a, pos_b, w1a, w1b, b1, w2, b2)


def _stack_heads(q):
    return jnp.concatenate([q[:, h * B_DK:(h + 1) * B_DK] for h in range(B_HPG)], axis=0)


LOG2E = 1.4426950408889634


def _head_slopes(g):
    return [LOG2E * jnp.exp2(jnp.full((1, LANES), -0.5, F32) * (g * B_HPG + h + 1).astype(F32))
            for h in range(B_HPG)]


def _lane_tiles(a):
    return [a[:, j * LANES:(j + 1) * LANES] for j in range(a.shape[1] // LANES)]


def _head_logits(s, h, tq, slope, rel, madd):
    rows = slice(h * tq, (h + 1) * tq)
    tiles = [s[rows, j * LANES:(j + 1) * LANES] for j in range(s.shape[1] // LANES)]
    if slope is None and madd is None:
        return tiles
    if slope is None:
        bias = _lane_tiles(madd)
    else:
        bias = [slope * r for r in _lane_tiles(rel)]
        if madd is not None:
            bias = [b_ + ma for b_, ma in zip(bias, _lane_tiles(madd))]
    return [t + b_ for t, b_ in zip(tiles, bias)]


def _with_ones(v):
    return jnp.concatenate([v, jnp.ones_like(v)], axis=1)


def _tile_max(tiles):
    mx = tiles[0]
    for t in tiles[1:]:
        mx = jnp.maximum(mx, t)
    return jnp.max(mx, axis=1, keepdims=True)


def _tile_sum(tiles):
    sm = tiles[0]
    for t in tiles[1:]:
        sm = sm + t
    return jnp.sum(sm, axis=1, keepdims=True)


def _nsa_cmp_body(q_ref, kc_ref, vc_ref, zg_ref, gl_ref, wov_ref,
                  oc_ref, sel_ref, any_ref, o_scr, imp_scr, *, tq, n_sel):
    g = pl.program_id(1)
    t0 = pl.program_id(2) * tq
    ncp = kc_ref.shape[2]
    nsp = wov_ref.shape[0]

    def attend(width):
        qs = _stack_heads(q_ref[...])
        slopes = _head_slopes(g)
        n = lax.broadcasted_iota(jnp.int32, (1, width), 1)
        rel = ((n * CMP_STRIDE - t0).astype(F32) + 0.5 * (CMP_BLK - 1))
        c_end = lax.broadcasted_iota(jnp.int32, (tq, width), 1) * CMP_STRIDE + (CMP_BLK - 1)
        tok = t0 + lax.broadcasted_iota(jnp.int32, (tq, width), 0)
        madd = jnp.where(c_end <= tok, 0.0, NEG)
        s = _dot_nt(qs, kc_ref[0, 0, 0:width, :])
        p_rows = []
        psum_tiles = None
        for h in range(B_HPG):
            tiles = _head_logits(s, h, tq, slopes[h], rel, madd)
            m = jnp.maximum(_tile_max(tiles), M_INIT)
            es = [jnp.exp2(t - m) for t in tiles]
            inv = 1.0 / jnp.maximum(_tile_sum(es), 1e-30)
            ps = [e * inv for e in es]
            psum_tiles = ps if psum_tiles is None else [a + b_ for a, b_ in zip(psum_tiles, ps)]
            p_rows.append(jnp.concatenate([p.astype(BF16) for p in ps], axis=1))
        o_scr[...] = _dot(jnp.concatenate(p_rows, axis=0), vc_ref[0, 0, 0:width, :])
        psum = jnp.concatenate(psum_tiles, axis=1)
        p_hi = psum.astype(BF16)
        p_lo = (psum - p_hi.astype(F32)).astype(BF16)
        wov = wov_ref[:, 0:width]
        imp_scr[...] = _dot_nt(wov, p_hi) + _dot_nt(wov, p_lo)

    ntile = ncp // LANES
    need = _shr(t0 + tq - 1, CMP_STRIDE * LANES) + 1
    for k in range(1, ntile + 1):
        @pl.when((need == k) if k < ntile else (need >= k))
        def _(k=k):
            attend(k * LANES)

    o = o_scr[...]
    imp = imp_scr[...]
    j = lax.broadcasted_iota(jnp.int32, (nsp, tq), 0)
    cur = _shr(t0 + lax.broadcasted_iota(jnp.int32, (1, tq), 1), SEL_BLK)
    forced = (j == 0) | (j == cur) | (j == cur - 1)
    val = jnp.where(forced, -jnp.inf, jnp.where(j <= cur, imp, -jnp.inf))
    sel_t = jnp.where(forced, 1.0, 0.0)
    assert n_sel >= 3
    for _ in range(n_sel - 3):
        mx = jnp.max(val, axis=0, keepdims=True)
        cand = jnp.where((val == mx) & (mx > -jnp.inf), j, nsp)
        pick = j == jnp.min(cand, axis=0, keepdims=True)
        sel_t = jnp.where(pick, 1.0, sel_t)
        val = jnp.where(pick, -jnp.inf, val)
    sel = sel_t.T
    sel_ref[0, 0] = sel.astype(sel_ref.dtype)
    any_ref[0, 0, 0] = jnp.broadcast_to(jnp.max(sel, axis=0, keepdims=True), (SUBLANES, nsp))
    gate = gl_ref[...]
    for h in range(B_HPG):
        hs = slice(h * B_DK, (h + 1) * B_DK)
        oc_ref[:, hs] = gate[:, h:h + 1] * zg_ref[:, hs] * o[h * tq:(h + 1) * tq]


def _nsa_cmp(q, kvc, zg, gates, wov_t, batch, seq, n_sel):
    t = q.shape[0]
    tq = Q_TILE
    nqb = seq // tq
    ncp = kvc.shape[2]
    nsp = wov_t.shape[0]
    gw = B_HPG * B_DK
    return pl.pallas_call(
        functools.partial(_nsa_cmp_body, tq=tq, n_sel=n_sel),
        grid=(batch, B_GROUPS, nqb),
        in_specs=[pl.BlockSpec((tq, gw), lambda b, g, i: (b * nqb + i, g)),
                  pl.BlockSpec((1, 1, ncp, B_DK), lambda b, g, i: (g, b, 0, 0)),
                  pl.BlockSpec((1, 1, ncp, B_DK), lambda b, g, i: (B_GROUPS + g, b, 0, 0)),
                  pl.BlockSpec((tq, gw), lambda b, g, i: (b * nqb + i, g)),
                  pl.BlockSpec((tq, LANES), lambda b, g, i: (b * nqb + i, g)),
                  pl.BlockSpec((nsp, ncp), lambda b, g, i: (0, 0))],
        out_specs=[pl.BlockSpec((tq, gw), lambda b, g, i: (b * nqb + i, g)),
                   pl.BlockSpec((1, 1, tq, nsp), lambda b, g, i: (b, g, i, 0)),
                   pl.BlockSpec((1, 1, 1, SUBLANES, nsp), lambda b, g, i: (b, g, i, 0, 0))],
        out_shape=[jax.ShapeDtypeStruct((t, B_GROUPS * gw), F32),
                   jax.ShapeDtypeStruct((batch, B_GROUPS, seq, nsp), BF16),
                   jax.ShapeDtypeStruct((batch, B_GROUPS, nqb, SUBLANES, nsp), F32)],
        scratch_shapes=[pltpu.VMEM((B_HPG * tq, B_DK), F32), pltpu.VMEM((nsp, tq), F32)],
        compiler_params=_params(3),
        name="nsa_cmp",
    )(q, kvc, kvc, zg, gates, wov_t)


def _softmax_once(s, v, madd, tq):
    ms = [jnp.maximum(_tile_max(_head_logits(s, h, tq, None, None, madd)), M_INIT)
          for h in range(B_HPG)]
    p_rows = []
    for h in range(B_HPG):
        tiles = _head_logits(s, h, tq, None, None, madd)
        p_rows.append(jnp.concatenate([jnp.exp2(t - ms[h]).astype(BF16) for t in tiles], axis=1))
    pv = _dot(jnp.concatenate(p_rows, axis=0), _with_ones(v))
    return pv[:, :B_DK] * (1.0 / jnp.maximum(pv[:, B_DK:], 1e-30))


def _online_update(s, v_c, slopes, rel, madd, m_ref, l_ref, acc_ref, tq):
    slope_of = (lambda h: None) if slopes is None else (lambda h: slopes[h])
    p_rows = []
    alphas = []
    m_news = []
    for h in range(B_HPG):
        rows = slice(h * tq, (h + 1) * tq)
        m_prev = m_ref[rows, :]
        m_new = jnp.maximum(m_prev, _tile_max(_head_logits(s, h, tq, slope_of(h), rel, madd)))
        m_ref[rows, :] = m_new
        alphas.append(jnp.exp2(m_prev - m_new))
        m_news.append(m_new)
    for h in range(B_HPG):
        tiles = _head_logits(s, h, tq, slope_of(h), rel, madd)
        p_rows.append(jnp.concatenate([jnp.exp2(t - m_news[h]).astype(BF16) for t in tiles], axis=1))
    pv = _dot(jnp.concatenate(p_rows, axis=0), _with_ones(v_c))
    for h in range(B_HPG):
        rows = slice(h * tq, (h + 1) * tq)
        acc_ref[rows, :] = alphas[h] * acc_ref[rows, :] + pv[rows, :B_DK]
        l_ref[rows, :] = alphas[h] * l_ref[rows, :] + pv[rows, B_DK:]


def _softmax_reset(m_ref, l_ref, acc_ref):
    m_ref[...] = jnp.full(m_ref.shape, M_INIT, F32)
    l_ref[...] = jnp.zeros(l_ref.shape, F32)
    acc_ref[...] = jnp.zeros(acc_ref.shape, F32)


def _softmax_result(l_ref, acc_ref):
    return acc_ref[...] * (1.0 / jnp.maximum(l_ref[...], 1e-30))


def _nsa_sw_body(lst_ref, q_ref, sel_ref, ks_ref, vs_ref, kw_ref, vw_ref,
                 zs_ref, zw_ref, gl_ref, oc_ref, out_ref,
                 qs_ref, s_a, s_b, sw_ref, m_ref, l_ref, acc_ref,
                 *, tq, nqb, nch):
    b = pl.program_id(0)
    g = pl.program_id(1)
    i = pl.program_id(2)
    t0 = i * tq
    qs_ref[:, :B_DK] = _stack_heads(q_ref[...])
    not_sel = (1.0 - sel_ref[0, 0].astype(F32)).astype(BF16)
    qs_ref[:, B_DK:] = jnp.concatenate([not_sel] * B_HPG, axis=0)
    slopes = _head_slopes(g)
    nsp = sel_ref.shape[3]
    ch = SEL_CHUNK
    bpc = ch // SEL_BLK
    lbase = ((b * B_GROUPS + g) * nqb + i) * (nch + 2)
    n_reg = lst_ref[lbase] - 1

    def positions(start, width):
        rel = (start - t0) + lax.broadcasted_iota(jnp.int32, (1, width), 1)
        dist = lax.broadcasted_iota(jnp.int32, (tq, width), 0) - (
            (start - t0) + lax.broadcasted_iota(jnp.int32, (tq, width), 1))
        return rel.astype(F32), dist

    def scores(c):
        start = pl.multiple_of(c * ch, ch)
        key_blk = c * bpc + _shr(lax.broadcasted_iota(jnp.int32, (ch, nsp), 0), SEL_BLK)
        own_blk = jnp.where(lax.broadcasted_iota(jnp.int32, (ch, nsp), 1) == key_blk, MASK_NEG, 0.0)
        keys = jnp.concatenate([ks_ref[0, 0, pl.ds(start, ch), :], own_blk.astype(BF16)], axis=1)
        return _dot_nt(qs_ref[...], keys)

    def stage(j, s_ref):
        c = lst_ref[lbase + 1 + j]
        s = scores(c)
        rel, _ = positions(c * ch, ch)
        for h in range(B_HPG):
            for jt, t in enumerate(_head_logits(s, h, tq, slopes[h], rel, None)):
                s_ref[h * tq:(h + 1) * tq, jt * LANES:(jt + 1) * LANES] = t

    def consume(j, s_ref):
        start = pl.multiple_of(lst_ref[lbase + 1 + j] * ch, ch)
        _online_update(s_ref, vs_ref[0, 0, pl.ds(start, ch), :], None, None, None,
                       m_ref, l_ref, acc_ref, tq)

    nwin = WINDOW // ch + max(tq // ch, 1)
    c_lo = jnp.maximum(_shr(t0 + tq - 1, ch) - (nwin - 1), 0)
    st = pl.multiple_of(c_lo * ch, ch)

    def consume_own(s_ref):
        sw = _dot_nt(qs_ref[:, :B_DK], kw_ref[0, 0, pl.ds(st, nwin * ch), :])
        rel_w, _ = positions(st, nwin * ch)
        for h in range(B_HPG):
            for jt, t in enumerate(_head_logits(sw, h, tq, slopes[h], rel_w, None)):
                sw_ref[h * tq:(h + 1) * tq, jt * LANES:(jt + 1) * LANES] = t
        start = pl.multiple_of(lst_ref[lbase + 1 + n_reg] * ch, ch)
        _, dist = positions(start, ch)
        _online_update(s_ref, vs_ref[0, 0, pl.ds(start, ch), :], None, None,
                       jnp.where(dist >= 0, 0.0, NEG), m_ref, l_ref, acc_ref, tq)

    _softmax_reset(m_ref, l_ref, acc_ref)
    stage(0, s_a)

    def chunk_pair(jj, carry):
        j = 2 * jj
        stage(j + 1, s_b)
        consume(j, s_a)

        @pl.when(j + 1 < n_reg)
        def _():
            stage(j + 2, s_a)
            consume(j + 1, s_b)
        return carry

    lax.fori_loop(0, _shr(n_reg + 1, 2), chunk_pair, 0)

    @pl.when((n_reg & 1) == 0)
    def _():
        consume_own(s_a)

    @pl.when((n_reg & 1) == 1)
    def _():
        consume_own(s_b)

    o_s = _softmax_result(l_ref, acc_ref)

    _, dist = positions(st, nwin * ch)
    madd = jnp.where(dist >= 0, jnp.where(dist < WINDOW, 0.0, NEG), NEG)
    o_w = _softmax_once(sw_ref, vw_ref[0, 0, pl.ds(st, nwin * ch), :], madd, tq)

    gate = gl_ref[...]
    for h in range(B_HPG):
        hs = slice(h * B_DK, (h + 1) * B_DK)
        rs = slice(h * tq, (h + 1) * tq)
        merged = (oc_ref[:, hs]
                  + gate[:, B_HPG + h:B_HPG + h + 1] * zs_ref[:, hs] * o_s[rs]
                  + gate[:, 2 * B_HPG + h:2 * B_HPG + h + 1] * zw_ref[:, hs] * o_w[rs])
        out_ref[:, hs] = merged.astype(out_ref.dtype)


def _nsa_sw(flags, q, sel, kv_att, zg, gates, oc, batch, seq):
    t = q.shape[0]
    tq = Q_TILE
    nqb = seq // tq
    nch = seq // SEL_CHUNK
    nsp = sel.shape[3]
    gw = B_HPG * B_DK
    rows = B_HPG * tq
    tok = lambda j: pl.BlockSpec((tq, gw), lambda b, g, i, f, j=j: (b * nqb + i, j * B_GROUPS + g))
    res = lambda k: pl.BlockSpec((1, 1, seq, B_DK), lambda b, g, i, f, k=k: (k * B_GROUPS + g, b, 0, 0))
    grid_spec = pltpu.PrefetchScalarGridSpec(
        num_scalar_prefetch=1,
        grid=(batch, B_GROUPS, nqb),
        in_specs=[tok(0),
                  pl.BlockSpec((1, 1, tq, nsp), lambda b, g, i, f: (b, g, i, 0)),
                  res(0), res(1), res(2), res(3),
                  tok(1), tok(2),
                  pl.BlockSpec((tq, LANES), lambda b, g, i, f: (b * nqb + i, g)),
                  tok(0)],
        out_specs=tok(0),
        scratch_shapes=[pltpu.VMEM((rows, B_DK + nsp), BF16),
                        pltpu.VMEM((rows, SEL_CHUNK), F32),
                        pltpu.VMEM((rows, SEL_CHUNK), F32),
                        pltpu.VMEM((rows, WINDOW + max(tq, SEL_CHUNK)), F32),
                        pltpu.VMEM((rows, LANES), F32),
                        pltpu.VMEM((rows, LANES), F32),
                        pltpu.VMEM((rows, B_DK), F32)])
    return pl.pallas_call(
        functools.partial(_nsa_sw_body, tq=tq, nqb=nqb, nch=nch),
        grid_spec=grid_spec,
        out_shape=jax.ShapeDtypeStruct((t, B_GROUPS * gw), BF16),
        compiler_params=_params(3),
        name="nsa_sw",
    )(flags, q, sel, kv_att, kv_att, kv_att, kv_att, zg, zg, gates, oc)


def _blockdiag_dense(w):
    nb, bs, _ = w.shape
    per = MXU_DIM // bs
    w4 = w.reshape(nb // per, per, bs, bs)
    dense = jnp.einsum("sgij,gh->sgihj", w4, jnp.eye(per, dtype=w.dtype))
    return dense.reshape(nb // per, MXU_DIM, MXU_DIM).astype(BF16)


def _pad_cols(w, width):
    return jnp.pad(w, ((0, 0), (0, width - w.shape[1])))


def _overlap_matrix_t(nr, nsp, nc, ns):
    i = jnp.arange(nr)[None, :] * CMP_STRIDE
    j = jnp.arange(nsp)[:, None] * SEL_BLK
    ov = jnp.minimum(i + CMP_BLK, j + SEL_BLK) - jnp.maximum(i, j)
    ov = jnp.maximum(ov, 0) / CMP_STRIDE
    live = (jnp.arange(nr)[None, :] < nc) & (jnp.arange(nsp)[:, None] < ns)
    return jnp.where(live, ov, 0).astype(BF16)


def _mlstm_layer(xf, batch, seq, g_pre, g_post, w_in, conv_w, conv_b, w_q, w_k, w_v,
                 w_gate, b_gate, head_norm, skip, w_out):
    c = conv_w.shape[1]
    dh = c // A_HEADS
    proj = _rms_matmul(xf, g_pre, w_in.astype(BF16), F32)
    wg = [_pad_cols(w_gate[j * c:(j + 1) * c], LANES).astype(BF16) for j in range(3)]
    bg = _pad_cols(b_gate.reshape(1, -1), LANES).astype(F32)
    q, k, v, xc, gates = _a_local(
        proj, seq, conv_w.astype(F32), conv_b.reshape(1, c).astype(F32),
        _blockdiag_dense(w_q), _blockdiag_dense(w_k), _blockdiag_dense(w_v),
        wg[0], wg[1], wg[2], bg, float(dh) ** -0.5)
    y = _a_cell(q, k, v, xc, proj, gates, head_norm.reshape(1, c).astype(F32),
                skip.reshape(1, c).astype(F32), batch, seq)
    return _out_proj(y, w_out.astype(BF16), g_post, xf)


def _shared_kv(xf, batch, seq, kv_norm, kv_w, cmp_pos, cmp_w1, cmp_b1, cmp_w2, cmp_b2):
    ncmp = 2 * B_GROUPS * B_DK
    kv_w = kv_w.astype(BF16)
    kv_cmp, h = _rms_matmul(xf, kv_norm, kv_w[:, :ncmp], F32, split=True, emit_h=True)
    kv_att = _matmul(h, kv_w[:, ncmp:], BF16, split=True)
    nr = seq // CMP_STRIDE
    half = CMP_STRIDE * B_DK
    pos_a = cmp_pos[:, :CMP_STRIDE].reshape(2, 1, half).astype(F32)
    pos_b = cmp_pos[:, CMP_STRIDE:].reshape(2, 1, half).astype(F32)
    kvc = _kv_compress(kv_cmp.reshape(2 * B_GROUPS, batch, nr, half), pos_a, pos_b,
                       cmp_w1[:, :half].astype(BF16), cmp_w1[:, half:].astype(BF16),
                       cmp_b1.reshape(2, 1, -1).astype(F32), cmp_w2.astype(BF16),
                       cmp_b2.reshape(2, 1, -1).astype(F32))
    return kvc, kv_att.reshape(4 * B_GROUPS, batch, seq, B_DK)


def _nsa_layer(xf, batch, seq, g_pre, g_post, w_in, w_out, shared, wov_t, n_sel):
    kvc, kv_att = shared
    nqk = B_HEADS * B_DK
    ngl = N_BRANCH * B_HEADS
    w_q = w_in[:, :nqk].astype(BF16)
    w_gl = w_in[:, nqk:nqk + ngl].reshape(-1, N_BRANCH, B_GROUPS, B_HPG)
    w_gl = jnp.transpose(w_gl, (0, 2, 1, 3)).reshape(-1, B_GROUPS, N_BRANCH * B_HPG)
    w_gl = jnp.pad(w_gl, ((0, 0), (0, 0), (0, LANES - N_BRANCH * B_HPG))).reshape(-1, B_GROUPS * LANES)
    q, h = _rms_matmul(xf, g_pre, w_q, BF16, scale=LOG2E * float(B_DK) ** -0.5, emit_h=True)
    zg = _matmul(h, w_in[:, nqk + ngl:].astype(BF16), F32, act="silu")
    gates = _matmul(h, w_gl.astype(BF16), F32, act="sigmoid")
    oc, sel, blk_any = _nsa_cmp(q, kvc, zg, gates, wov_t, batch, seq, n_sel)
    ns = seq // SEL_BLK
    bpc = SEL_CHUNK // SEL_BLK
    flags = blk_any[:, :, :, 0, :ns].reshape(batch, B_GROUPS, seq // Q_TILE, ns // bpc, bpc)
    flags = (jnp.max(flags, axis=-1) > 0).astype(jnp.int32)
    order = jnp.argsort(1 - flags, axis=-1, stable=True).astype(jnp.int32)
    count = jnp.sum(flags, axis=-1, keepdims=True)
    lists = jnp.concatenate([count, order, jnp.zeros_like(count)], axis=-1).reshape(-1)
    out = _nsa_sw(lists, q, sel, kv_att, zg, gates, oc, batch, seq)
    return _out_proj(out, w_out.astype(BF16), g_post, xf)


def kernel(x, norm_pre, norm_post, a_w_in, a_conv_w, a_conv_b, a_w_q, a_w_k, a_w_v, a_w_gate,
           a_b_gate, a_head_norm, a_skip, a_w_out, kv_norm, kv_w, cmp_pos, cmp_w1, cmp_b1,
           cmp_w2, cmp_b2, b_w_in, b_w_out):
    batch, seq, d = x.shape
    na = a_w_in.shape[0]
    nb = b_w_in.shape[0]
    assert seq % (CMP_STRIDE * LANES) == 0 and seq % SEL_CHUNK == 0 and seq % A_CHUNK == 0
    xf = x.reshape(batch * seq, d).astype(F32)
    for l in range(na):
        xf = _mlstm_layer(xf, batch, seq, norm_pre[l], norm_post[l], a_w_in[l], a_conv_w[l],
                          a_conv_b[l], a_w_q[l], a_w_k[l], a_w_v[l], a_w_gate[l], a_b_gate[l],
                          a_head_norm[l], a_skip[l], a_w_out[l])
    shared = _shared_kv(xf, batch, seq, kv_norm, kv_w, cmp_pos, cmp_w1, cmp_b1, cmp_w2, cmp_b2)
    nr = seq // CMP_STRIDE
    ns = seq // SEL_BLK
    nsp = -(-ns // LANES) * LANES
    wov_t = _overlap_matrix_t(nr, nsp, nr - CMP_BLK // CMP_STRIDE + 1, ns)
    for l in range(nb):
        xf = _nsa_layer(xf, batch, seq, norm_pre[na + l], norm_post[na + l], b_w_in[l],
                        b_w_out[l], shared, wov_t, min(SEL_TOP, ns))
    return xf.reshape(batch, seq, d).astype(x.dtype)
```

```python
import functools

import jax
import jax.numpy as jnp
from jax import lax
from jax.experimental import pallas as pl
from jax.experimental.pallas import tpu as pltpu

F32 = jnp.float32
BF16 = jnp.bfloat16
EPS = 1e-6

A_HEADS = 4
A_CONV = 4
B_HEADS = 16
B_GROUPS = 2
B_HPG = B_HEADS // B_GROUPS
B_DK = 128
N_BRANCH = 3
CMP_BLK = 32
CMP_STRIDE = 16
CMP_HIDDEN = 256
SEL_BLK = 64
SEL_TOP = 16
WINDOW = 512

LANES = 128
SUBLANES = 8
MXU_DIM = 256
VMEM_LIMIT_BYTES = 56 * 1024 * 1024

A_CHUNK = 256
A_LOCAL_TILE = 256
Q_TILE = 256
SEL_CHUNK = 256
NEG = -1e30
M_INIT = -1e20
MASK_NEG = -2.0 ** 100
assert B_DK == LANES


def _params(n_axes):
    return pltpu.CompilerParams(
        dimension_semantics=("arbitrary",) * n_axes,
        vmem_limit_bytes=VMEM_LIMIT_BYTES)


def _sigmoid(v):
    return 1.0 / (1.0 + jnp.exp(-v))


def _silu(v):
    return v * _sigmoid(v)


def _shr(v, pow2):
    assert pow2 & (pow2 - 1) == 0
    return lax.shift_right_logical(v, jnp.int32(pow2.bit_length() - 1))


def _dot(a, b):
    return jnp.dot(a, b, preferred_element_type=F32)


def _dot_nt(a, b):
    return lax.dot_general(a, b, (((1,), (1,)), ((), ())), preferred_element_type=F32)


def _dot_tn(a, b):
    return lax.dot_general(a, b, (((0,), (0,)), ((), ())), preferred_element_type=F32)


_ACTS = {None: lambda v: v, "sigmoid": _sigmoid, "silu": _silu}


def _store_proj(o_ref, acc, scale, act):
    if scale != 1.0:
        acc = acc * scale
    val = _ACTS[act](acc)
    if len(o_ref.shape) == 2:
        o_ref[...] = val.astype(o_ref.dtype)
    else:
        for j in range(o_ref.shape[0]):
            o_ref[j] = val[:, j * LANES:(j + 1) * LANES].astype(o_ref.dtype)


def _rms_matmul_body(x_ref, g_ref, w_ref, o_ref, *h_ref, scale):
    x = x_ref[...]
    ms = jnp.mean(x * x, axis=-1, keepdims=True)
    h = ((x * lax.rsqrt(ms + EPS)) * g_ref[...]).astype(BF16)
    if h_ref:
        h_ref[0][...] = h
    _store_proj(o_ref, _dot(h, w_ref[...]), scale, None)


def _matmul_body(h_ref, w_ref, o_ref, *, act):
    _store_proj(o_ref, _dot(h_ref[...], w_ref[...]), 1.0, act)


def _pick_tile(n, candidates):
    for c in candidates:
        if n % c == 0:
            return c
    return n


def _proj_tiles(t, n, split, out_dtype):
    tm = _pick_tile(t, (1024, 512, 256, 128))
    tn = _pick_tile(n, (2048, 1536, 1280, 1024, 896, 768, 512, 256, 128))
    if split:
        out_spec = pl.BlockSpec((tn // LANES, tm, LANES), lambda j, i: (j, i, 0))
        out_shape = jax.ShapeDtypeStruct((n // LANES, t, LANES), out_dtype)
    else:
        out_spec = pl.BlockSpec((tm, tn), lambda j, i: (i, j))
        out_shape = jax.ShapeDtypeStruct((t, n), out_dtype)
    return tm, tn, out_spec, out_shape


def _rms_matmul(x, g, w, out_dtype, scale=1.0, split=False, emit_h=False):
    t, d = x.shape
    n = w.shape[1]
    tm, tn, out_spec, out_shape = _proj_tiles(t, n, split, out_dtype)
    out_specs, out_shapes = [out_spec], [out_shape]
    if emit_h:
        assert n == tn
        out_specs.append(pl.BlockSpec((tm, d), lambda j, i: (i, 0)))
        out_shapes.append(jax.ShapeDtypeStruct((t, d), BF16))
    res = pl.pallas_call(
        functools.partial(_rms_matmul_body, scale=scale),
        grid=(n // tn, t // tm),
        in_specs=[pl.BlockSpec((tm, d), lambda j, i: (i, 0)),
                  pl.BlockSpec((1, d), lambda j, i: (0, 0)),
                  pl.BlockSpec((d, tn), lambda j, i: (0, j))],
        out_specs=out_specs,
        out_shape=out_shapes,
        compiler_params=_params(2),
        name="rms_matmul",
    )(x, g.reshape(1, d).astype(F32), w)
    return res if emit_h else res[0]


def _matmul(h, w, out_dtype, split=False, act=None):
    t, d = h.shape
    n = w.shape[1]
    tm, tn, out_spec, out_shape = _proj_tiles(t, n, split, out_dtype)
    return pl.pallas_call(
        functools.partial(_matmul_body, act=act),
        grid=(n // tn, t // tm),
        in_specs=[pl.BlockSpec((tm, d), lambda j, i: (i, 0)),
                  pl.BlockSpec((d, tn), lambda j, i: (0, j))],
        out_specs=out_spec,
        out_shape=out_shape,
        compiler_params=_params(2),
        name="matmul",
    )(h, w)


def _out_proj_body(y_ref, w_ref, g_ref, x_ref, o_ref):
    r = _dot(y_ref[...], w_ref[...])
    ms = jnp.mean(r * r, axis=-1, keepdims=True)
    o_ref[...] = x_ref[...] + (r * lax.rsqrt(ms + EPS)) * g_ref[...]


def _out_proj(y, w, g, x):
    t, k = y.shape
    d = w.shape[1]
    tm = _pick_tile(t, (1024, 512, 256, 128))
    return pl.pallas_call(
        _out_proj_body,
        grid=(t // tm,),
        in_specs=[pl.BlockSpec((tm, k), lambda i: (i, 0)),
                  pl.BlockSpec((k, d), lambda i: (0, 0)),
                  pl.BlockSpec((1, d), lambda i: (0, 0)),
                  pl.BlockSpec((tm, d), lambda i: (i, 0))],
        out_specs=pl.BlockSpec((tm, d), lambda i: (i, 0)),
        out_shape=jax.ShapeDtypeStruct((t, d), F32),
        compiler_params=_params(1),
        name="out_proj",
    )(y, w, g.reshape(1, d).astype(F32), x)


HALO = SUBLANES


def _a_local_body(xm_ref, cw_ref, cb_ref, bdq_ref, bdk_ref, bdv_ref,
                  wgq_ref, wgk_ref, wgv_ref, bg_ref,
                  q_ref, k_ref, v_ref, xc_ref, gates_ref, ext_ref, cv_ref,
                  *, tl, tiles_per_seq, kscale):
    i = pl.program_id(0)
    c = xm_ref.shape[1]
    first = lax.rem(i, tiles_per_seq) == 0

    @pl.when(first)
    def _():
        ext_ref[0:HALO, :] = jnp.zeros((HALO, c), F32)

    @pl.when(jnp.logical_not(first))
    def _():
        ext_ref[0:HALO, :] = ext_ref[tl:tl + HALO, :]

    xm = xm_ref[...]
    ext_ref[HALO:HALO + tl, :] = xm
    acc = jnp.broadcast_to(cb_ref[...], (tl, c))
    for j in range(A_CONV):
        off = HALO - (A_CONV - 1) + j
        acc = acc + cw_ref[j:j + 1, :] * ext_ref[off:off + tl, :]
    cv_ref[...] = acc
    xc = _silu(cv_ref[...])
    xcb = xc.astype(BF16)
    xmb = xm.astype(BF16)
    xc_ref[...] = xcb
    gates = jnp.broadcast_to(bg_ref[...], (tl, LANES))
    for s in range(c // MXU_DIM):
        sl = slice(MXU_DIM * s, MXU_DIM * (s + 1))
        qb = _dot(xcb[:, sl], bdq_ref[s]).astype(BF16)
        kf = _dot(xcb[:, sl], bdk_ref[s])
        kb = kf.astype(BF16)
        vb = _dot(xmb[:, sl], bdv_ref[s]).astype(BF16)
        gates = gates + _dot(qb, wgq_ref[sl, :]) + _dot(kb, wgk_ref[sl, :]) + _dot(vb, wgv_ref[sl, :])
        q_ref[:, sl] = qb
        k_ref[:, sl] = (kf * kscale).astype(BF16)
        v_ref[:, sl] = vb
    gates_ref[...] = gates


def _a_local(xm_src, seq, cw, cb, bdq, bdk, bdv, wgq, wgk, wgv, bg, kscale):
    t = xm_src.shape[0]
    c = cw.shape[1]
    tl = A_LOCAL_TILE
    nslab = c // MXU_DIM
    full = lambda shape: pl.BlockSpec(shape, lambda i: (0,) * len(shape))
    tok = lambda w: pl.BlockSpec((tl, w), lambda i: (i, 0))
    return pl.pallas_call(
        functools.partial(_a_local_body, tl=tl, tiles_per_seq=seq // tl, kscale=kscale),
        grid=(t // tl,),
        in_specs=[tok(c), full((A_CONV, c)), full((1, c)),
                  full((nslab, MXU_DIM, MXU_DIM)), full((nslab, MXU_DIM, MXU_DIM)),
                  full((nslab, MXU_DIM, MXU_DIM)),
                  full((c, LANES)), full((c, LANES)), full((c, LANES)), full((1, LANES))],
        out_specs=[tok(c), tok(c), tok(c), tok(c), tok(LANES)],
        out_shape=[jax.ShapeDtypeStruct((t, c), BF16)] * 4 + [jax.ShapeDtypeStruct((t, LANES), F32)],
        scratch_shapes=[pltpu.VMEM((tl + 2 * HALO, c), F32), pltpu.VMEM((tl, c), F32)],
        compiler_params=_params(1),
        name="a_local",
    )(xm_src, cw, cb, bdq, bdk, bdv, wgq, wgk, wgv, bg)


def _a_cell_body(q_ref, k_ref, v_ref, xc_ref, o_ref, z_ref, gates_ref, hn_ref, skip_ref,
                 y_ref, c_ref, m_ref, *, chunk, dh):
    cidx = pl.program_id(1)
    nh = A_HEADS

    @pl.when(cidx == 0)
    def _():
        c_ref[...] = jnp.zeros(c_ref.shape, F32)
        m_ref[...] = jnp.zeros(m_ref.shape, F32)

    gates = gates_ref[...]
    lf = jnp.minimum(gates, 0.0) - jnp.log(1.0 + jnp.exp(-jnp.abs(gates)))
    r_i = lax.broadcasted_iota(jnp.int32, (chunk, chunk), 0)
    c_i = lax.broadcasted_iota(jnp.int32, (chunk, chunk), 1)
    causal = r_i >= c_i
    tril = causal.astype(BF16)
    hi = lf.astype(BF16)
    r1 = lf - hi.astype(F32)
    mid = r1.astype(BF16)
    lo = (r1 - mid.astype(F32)).astype(BF16)
    b_col = _dot(tril, hi) + _dot(tril, mid) + _dot(tril, lo)
    g_t = gates.T
    b_t = b_col.T

    for h in range(nh):
        hs = slice(h * dh, (h + 1) * dh)
        qh = q_ref[:, hs]
        kh = k_ref[:, hs]
        vh = v_ref[:, hs]
        i_row = g_t[h:h + 1, :]
        b_row = b_t[nh + h:nh + h + 1, :]
        i_c = gates[:, h:h + 1]
        b_c = b_col[:, nh + h:nh + h + 1]
        m_b = m_ref[h:h + 1, 0:1]
        log_d = jnp.where(causal, b_c - b_row + i_row, -jnp.inf)
        m_inter = b_c + m_b
        m_t = jnp.maximum(jnp.max(log_d, axis=1, keepdims=True), m_inter)
        d = jnp.exp(log_d - m_t)
        s = _dot_nt(qh, kh) * d
        decay = jnp.exp(m_inter - m_t)
        inter = _dot(qh, c_ref[h].astype(BF16))
        intra = _dot(s.astype(BF16), vh)
        num = decay * inter[:, :dh] + intra
        den = decay * inter[:, dh:] + jnp.sum(s, axis=1, keepdims=True)
        scale = 1.0 / jnp.maximum(jnp.abs(den), jnp.exp(-m_t))
        b_last = b_col[chunk - 1:chunk, nh + h:nh + h + 1]
        m_new = jnp.maximum(b_last + m_b,
                            jnp.max(b_last - b_row + i_row, axis=1, keepdims=True))
        w_c = jnp.exp(b_last - b_c + i_c - m_new)
        carry = jnp.exp(b_last + m_b - m_new)
        wv = jnp.concatenate([vh.astype(F32) * w_c, jnp.broadcast_to(w_c, (chunk, LANES))],
                             axis=1).astype(BF16)
        c_ref[h] = carry * c_ref[h] + _dot_tn(kh, wv)
        m_ref[h:h + 1, :] = jnp.broadcast_to(m_new, (1, LANES))
        lanes = [slice(h * dh + j * LANES, h * dh + (j + 1) * LANES) for j in range(dh // LANES)]
        hc = [_sigmoid(o_ref[:, ls]) * (num[:, j * LANES:(j + 1) * LANES] * scale)
              for j, ls in enumerate(lanes)]
        mu = _tile_sum(hc) * (1.0 / dh)
        cen = [t - mu for t in hc]
        rstd = lax.rsqrt(_tile_sum([t * t for t in cen]) * (1.0 / dh) + EPS)
        for t, ls in zip(cen, lanes):
            y = (t * rstd * hn_ref[:, ls] + skip_ref[:, ls] * xc_ref[:, ls].astype(F32)) * _silu(z_ref[:, ls])
            y_ref[:, ls] = y.astype(BF16)


def _a_cell(q, k, v, xc, proj, gates, head_norm, skip, batch, seq):
    t, c = q.shape
    chunk = A_CHUNK
    nc = seq // chunk
    dh = c // A_HEADS
    tokc = lambda j: pl.BlockSpec((chunk, c), lambda b, i, j=j: (b * nc + i, j))
    full = lambda shape: pl.BlockSpec(shape, lambda b, i: (0,) * len(shape))
    return pl.pallas_call(
        functools.partial(_a_cell_body, chunk=chunk, dh=dh),
        grid=(batch, nc),
        in_specs=[tokc(0), tokc(0), tokc(0), tokc(0), tokc(1), tokc(2),
                  pl.BlockSpec((chunk, LANES), lambda b, i: (b * nc + i, 0)),
                  full((1, c)), full((1, c))],
        out_specs=tokc(0),
        out_shape=jax.ShapeDtypeStruct((t, c), BF16),
        scratch_shapes=[pltpu.VMEM((A_HEADS, dh, dh + LANES), F32),
                        pltpu.VMEM((SUBLANES, LANES), F32)],
        compiler_params=_params(2),
        name="a_cell",
    )(q, k, v, xc, proj, proj, gates, head_norm, skip)


def _kv_compress_body(x_ref, pa_ref, pb_ref, w1a_ref, w1b_ref, b1_ref, w2_ref, b2_ref,
                      o_ref, nxt_ref):
    x = x_ref[0, 0]
    nr = x.shape[0]
    first = _dot((x + pa_ref[0]).astype(BF16), w1a_ref[0])
    nxt_ref[0:nr, :] = _dot((x + pb_ref[0]).astype(BF16), w1b_ref[0])
    nxt_ref[nr:nr + SUBLANES, :] = jnp.zeros((SUBLANES, nxt_ref.shape[1]), F32)
    h1 = _silu(first + nxt_ref[1:nr + 1, :] + b1_ref[0])
    o_ref[0, 0] = (_dot(h1.astype(BF16), w2_ref[0]) + b2_ref[0]).astype(o_ref.dtype)


def _kv_compress(x16, pos_a, pos_b, w1a, w1b, b1, w2, b2):
    nkg, batch, nr, half = x16.shape
    hid = w1a.shape[2]
    dk = w2.shape[2]
    kind = lambda shape: pl.BlockSpec((1,) + shape, lambda p, b: (p // B_GROUPS, 0, 0))
    return pl.pallas_call(
        _kv_compress_body,
        grid=(nkg, batch),
        in_specs=[pl.BlockSpec((1, 1, nr, half), lambda p, b: (p, b, 0, 0)),
                  kind((1, half)), kind((1, half)),
                  kind((half, hid)), kind((half, hid)), kind((1, hid)),
                  kind((hid, dk)), kind((1, dk))],
        out_specs=pl.BlockSpec((1, 1, nr, dk), lambda p, b: (p, b, 0, 0)),
        out_shape=jax.ShapeDtypeStruct((nkg, batch, nr, dk), BF16),
        scratch_shapes=[pltpu.VMEM((nr + SUBLANES, hid), F32)],
        compiler_params=_params(2),
        name="kv_compress",
    )(x16, pos_a, pos_b, w1a, w1b, b1, w2, b2)


def _stack_heads(q):
    return jnp.concatenate([q[:, h * B_DK:(h + 1) * B_DK] for h in range(B_HPG)], axis=0)


LOG2E = 1.4426950408889634


def _head_slopes(g):
    return [LOG2E * jnp.exp2(jnp.full((1, LANES), -0.5, F32) * (g * B_HPG + h + 1).astype(F32))
            for h in range(B_HPG)]


def _lane_tiles(a):
    return [a[:, j * LANES:(j + 1) * LANES] for j in range(a.shape[1] // LANES)]


def _head_logits(s, h, tq, slope, rel, madd):
    rows = slice(h * tq, (h + 1) * tq)
    tiles = [s[rows, j * LANES:(j + 1) * LANES] for j in range(s.shape[1] // LANES)]
    if slope is None and madd is None:
        return tiles
    if slope is None:
        bias = _lane_tiles(madd)
    else:
        bias = [slope * r for r in _lane_tiles(rel)]
        if madd is not None:
            bias = [b_ + ma for b_, ma in zip(bias, _lane_tiles(madd))]
    return [t + b_ for t, b_ in zip(tiles, bias)]


def _with_ones(v):
    return jnp.concatenate([v, jnp.ones_like(v)], axis=1)


def _tile_max(tiles):
    mx = tiles[0]
    for t in tiles[1:]:
        mx = jnp.maximum(mx, t)
    return jnp.max(mx, axis=1, keepdims=True)


def _tile_sum(tiles):
    sm = tiles[0]
    for t in tiles[1:]:
        sm = sm + t
    return jnp.sum(sm, axis=1, keepdims=True)


def _nsa_cmp_body(q_ref, kc_ref, vc_ref, zg_ref, gl_ref, wov_ref,
                  oc_ref, sel_ref, any_ref, o_scr, imp_scr, *, tq, n_sel):
    g = pl.program_id(1)
    t0 = pl.program_id(2) * tq
    ncp = kc_ref.shape[2]
    nsp = wov_ref.shape[0]

    def attend(width):
        qs = _stack_heads(q_ref[...])
        slopes = _head_slopes(g)
        n = lax.broadcasted_iota(jnp.int32, (1, width), 1)
        rel = ((n * CMP_STRIDE - t0).astype(F32) + 0.5 * (CMP_BLK - 1))
        c_end = lax.broadcasted_iota(jnp.int32, (tq, width), 1) * CMP_STRIDE + (CMP_BLK - 1)
        tok = t0 + lax.broadcasted_iota(jnp.int32, (tq, width), 0)
        madd = jnp.where(c_end <= tok, 0.0, NEG)
        s = _dot_nt(qs, kc_ref[0, 0, 0:width, :])
        p_rows = []
        psum_tiles = None
        for h in range(B_HPG):
            tiles = _head_logits(s, h, tq, slopes[h], rel, madd)
            m = jnp.maximum(_tile_max(tiles), M_INIT)
            es = [jnp.exp2(t - m) for t in tiles]
            inv = 1.0 / jnp.maximum(_tile_sum(es), 1e-30)
            ps = [e * inv for e in es]
            psum_tiles = ps if psum_tiles is None else [a + b_ for a, b_ in zip(psum_tiles, ps)]
            p_rows.append(jnp.concatenate([p.astype(BF16) for p in ps], axis=1))
        o_scr[...] = _dot(jnp.concatenate(p_rows, axis=0), vc_ref[0, 0, 0:width, :])
        psum = jnp.concatenate(psum_tiles, axis=1)
        p_hi = psum.astype(BF16)
        p_lo = (psum - p_hi.astype(F32)).astype(BF16)
        wov = wov_ref[:, 0:width]
        imp_scr[...] = _dot_nt(wov, p_hi) + _dot_nt(wov, p_lo)

    ntile = ncp // LANES
    need = _shr(t0 + tq - 1, CMP_STRIDE * LANES) + 1
    for k in range(1, ntile + 1):
        @pl.when((need == k) if k < ntile else (need >= k))
        def _(k=k):
            attend(k * LANES)

    o = o_scr[...]
    imp = imp_scr[...]
    j = lax.broadcasted_iota(jnp.int32, (nsp, tq), 0)
    cur = _shr(t0 + lax.broadcasted_iota(jnp.int32, (1, tq), 1), SEL_BLK)
    imp_scr[...] = jnp.where(j <= cur, 1.0, 0.0)

    @pl.when(t0 + tq > n_sel * SEL_BLK)
    def _():
        forced = (j == 0) | (j == cur) | (j == cur - 1)
        val = jnp.where(forced, -jnp.inf, jnp.where(j <= cur, imp, -jnp.inf))
        sel_t = jnp.where(forced, 1.0, 0.0)
        assert n_sel >= 3
        for _ in range(n_sel - 3):
            mx = jnp.max(val, axis=0, keepdims=True)
            cand = jnp.where((val == mx) & (mx > -jnp.inf), j, nsp)
            pick = j == jnp.min(cand, axis=0, keepdims=True)
            sel_t = jnp.where(pick, 1.0, sel_t)
            val = jnp.where(pick, -jnp.inf, val)
        imp_scr[...] = sel_t

    sel = imp_scr[...].T
    sel_ref[0, 0] = sel.astype(sel_ref.dtype)
    any_ref[0, 0, 0] = jnp.broadcast_to(jnp.max(sel, axis=0, keepdims=True), (SUBLANES, nsp))
    gate = gl_ref[...]
    for h in range(B_HPG):
        hs = slice(h * B_DK, (h + 1) * B_DK)
        oc_ref[:, hs] = gate[:, h:h + 1] * zg_ref[:, hs] * o[h * tq:(h + 1) * tq]


def _nsa_cmp(q, kvc, zg, gates, wov_t, batch, seq, n_sel):
    t = q.shape[0]
    tq = Q_TILE
    nqb = seq // tq
    ncp = kvc.shape[2]
    nsp = wov_t.shape[0]
    gw = B_HPG * B_DK
    return pl.pallas_call(
        functools.partial(_nsa_cmp_body, tq=tq, n_sel=n_sel),
        grid=(batch, B_GROUPS, nqb),
        in_specs=[pl.BlockSpec((tq, gw), lambda b, g, i: (b * nqb + i, g)),
                  pl.BlockSpec((1, 1, ncp, B_DK), lambda b, g, i: (g, b, 0, 0)),
                  pl.BlockSpec((1, 1, ncp, B_DK), lambda b, g, i: (B_GROUPS + g, b, 0, 0)),
                  pl.BlockSpec((tq, gw), lambda b, g, i: (b * nqb + i, g)),
                  pl.BlockSpec((tq, LANES), lambda b, g, i: (b * nqb + i, g)),
                  pl.BlockSpec((nsp, ncp), lambda b, g, i: (0, 0))],
        out_specs=[pl.BlockSpec((tq, gw), lambda b, g, i: (b * nqb + i, g)),
                   pl.BlockSpec((1, 1, tq, nsp), lambda b, g, i: (b, g, i, 0)),
                   pl.BlockSpec((1, 1, 1, SUBLANES, nsp), lambda b, g, i: (b, g, i, 0, 0))],
        out_shape=[jax.ShapeDtypeStruct((t, B_GROUPS * gw), F32),
                   jax.ShapeDtypeStruct((batch, B_GROUPS, seq, nsp), BF16),
                   jax.ShapeDtypeStruct((batch, B_GROUPS, nqb, SUBLANES, nsp), F32)],
        scratch_shapes=[pltpu.VMEM((B_HPG * tq, B_DK), F32), pltpu.VMEM((nsp, tq), F32)],
        compiler_params=_params(3),
        name="nsa_cmp",
    )(q, kvc, kvc, zg, gates, wov_t)


def _softmax_once(s, v, madd, tq):
    ms = [jnp.maximum(_tile_max(_head_logits(s, h, tq, None, None, madd)), M_INIT)
          for h in range(B_HPG)]
    p_rows = []
    for h in range(B_HPG):
        tiles = _head_logits(s, h, tq, None, None, madd)
        p_rows.append(jnp.concatenate([jnp.exp2(t - ms[h]).astype(BF16) for t in tiles], axis=1))
    pv = _dot(jnp.concatenate(p_rows, axis=0), _with_ones(v))
    return pv[:, :B_DK] * (1.0 / jnp.maximum(pv[:, B_DK:], 1e-30))


def _online_update(s, v_c, slopes, rel, madd, m_ref, l_ref, acc_ref, tq):
    slope_of = (lambda h: None) if slopes is None else (lambda h: slopes[h])
    p_rows = []
    alphas = []
    m_news = []
    for h in range(B_HPG):
        rows = slice(h * tq, (h + 1) * tq)
        m_prev = m_ref[rows, :]
        m_new = jnp.maximum(m_prev, _tile_max(_head_logits(s, h, tq, slope_of(h), rel, madd)))
        m_ref[rows, :] = m_new
        alphas.append(jnp.exp2(m_prev - m_new))
        m_news.append(m_new)
    for h in range(B_HPG):
        tiles = _head_logits(s, h, tq, slope_of(h), rel, madd)
        p_rows.append(jnp.concatenate([jnp.exp2(t - m_news[h]).astype(BF16) for t in tiles], axis=1))
    pv = _dot(jnp.concatenate(p_rows, axis=0), _with_ones(v_c))
    for h in range(B_HPG):
        rows = slice(h * tq, (h + 1) * tq)
        acc_ref[rows, :] = alphas[h] * acc_ref[rows, :] + pv[rows, :B_DK]
        l_ref[rows, :] = alphas[h] * l_ref[rows, :] + pv[rows, B_DK:]


def _softmax_reset(m_ref, l_ref, acc_ref):
    m_ref[...] = jnp.full(m_ref.shape, M_INIT, F32)
    l_ref[...] = jnp.zeros(l_ref.shape, F32)
    acc_ref[...] = jnp.zeros(acc_ref.shape, F32)


def _softmax_result(l_ref, acc_ref):
    return acc_ref[...] * (1.0 / jnp.maximum(l_ref[...], 1e-30))


def _nsa_sw_body(lst_ref, q_ref, sel_ref, ks_ref, vs_ref, kw_ref, vw_ref,
                 zs_ref, zw_ref, gl_ref, oc_ref, out_ref,
                 qs_ref, s_a, s_b, sw_ref, m_ref, l_ref, acc_ref,
                 *, tq, nqb, nch):
    b = pl.program_id(0)
    g = pl.program_id(1)
    i = pl.program_id(2)
    t0 = i * tq
    qs_ref[:, :B_DK] = _stack_heads(q_ref[...])
    not_sel = (1.0 - sel_ref[0, 0].astype(F32)).astype(BF16)
    qs_ref[:, B_DK:] = jnp.concatenate([not_sel] * B_HPG, axis=0)
    slopes = _head_slopes(g)
    nsp = sel_ref.shape[3]
    ch = SEL_CHUNK
    bpc = ch // SEL_BLK
    lbase = ((b * B_GROUPS + g) * nqb + i) * (nch + 2)
    n_reg = lst_ref[lbase] - 1

    def positions(start, width):
        rel = (start - t0) + lax.broadcasted_iota(jnp.int32, (1, width), 1)
        dist = lax.broadcasted_iota(jnp.int32, (tq, width), 0) - (
            (start - t0) + lax.broadcasted_iota(jnp.int32, (tq, width), 1))
        return rel.astype(F32), dist

    def scores(c):
        start = pl.multiple_of(c * ch, ch)
        key_blk = c * bpc + _shr(lax.broadcasted_iota(jnp.int32, (ch, nsp), 0), SEL_BLK)
        own_blk = jnp.where(lax.broadcasted_iota(jnp.int32, (ch, nsp), 1) == key_blk, MASK_NEG, 0.0)
        keys = jnp.concatenate([ks_ref[0, 0, pl.ds(start, ch), :], own_blk.astype(BF16)], axis=1)
        return _dot_nt(qs_ref[...], keys)

    def stage(j, s_ref):
        c = lst_ref[lbase + 1 + j]
        s = scores(c)
        rel, _ = positions(c * ch, ch)
        for h in range(B_HPG):
            for jt, t in enumerate(_head_logits(s, h, tq, slopes[h], rel, None)):
                s_ref[h * tq:(h + 1) * tq, jt * LANES:(jt + 1) * LANES] = t

    def consume(j, s_ref):
        start = pl.multiple_of(lst_ref[lbase + 1 + j] * ch, ch)
        _online_update(s_ref, vs_ref[0, 0, pl.ds(start, ch), :], None, None, None,
                       m_ref, l_ref, acc_ref, tq)

    nwin = WINDOW // ch + max(tq // ch, 1)
    c_lo = jnp.maximum(_shr(t0 + tq - 1, ch) - (nwin - 1), 0)
    st = pl.multiple_of(c_lo * ch, ch)

    def consume_own(s_ref):
        sw = _dot_nt(qs_ref[:, :B_DK], kw_ref[0, 0, pl.ds(st, nwin * ch), :])
        rel_w, _ = positions(st, nwin * ch)
        for h in range(B_HPG):
            for jt, t in enumerate(_head_logits(sw, h, tq, slopes[h], rel_w, None)):
                sw_ref[h * tq:(h + 1) * tq, jt * LANES:(jt + 1) * LANES] = t
        start = pl.multiple_of(lst_ref[lbase + 1 + n_reg] * ch, ch)
        _, dist = positions(start, ch)
        _online_update(s_ref, vs_ref[0, 0, pl.ds(start, ch), :], None, None,
                       jnp.where(dist >= 0, 0.0, NEG), m_ref, l_ref, acc_ref, tq)

    _softmax_reset(m_ref, l_ref, acc_ref)
    stage(0, s_a)

    def chunk_pair(jj, carry):
        j = 2 * jj
        stage(j + 1, s_b)
        consume(j, s_a)

        @pl.when(j + 1 < n_reg)
        def _():
            stage(j + 2, s_a)
            consume(j + 1, s_b)
        return carry

    lax.fori_loop(0, _shr(n_reg + 1, 2), chunk_pair, 0)

    @pl.when((n_reg & 1) == 0)
    def _():
        consume_own(s_a)

    @pl.when((n_reg & 1) == 1)
    def _():
        consume_own(s_b)

    o_s = _softmax_result(l_ref, acc_ref)

    _, dist = positions(st, nwin * ch)
    madd = jnp.where(dist >= 0, jnp.where(dist < WINDOW, 0.0, NEG), NEG)
    o_w = _softmax_once(sw_ref, vw_ref[0, 0, pl.ds(st, nwin * ch), :], madd, tq)

    gate = gl_ref[...]
    for h in range(B_HPG):
        hs = slice(h * B_DK, (h + 1) * B_DK)
        rs = slice(h * tq, (h + 1) * tq)
        merged = (oc_ref[:, hs]
                  + gate[:, B_HPG + h:B_HPG + h + 1] * zs_ref[:, hs] * o_s[rs]
                  + gate[:, 2 * B_HPG + h:2 * B_HPG + h + 1] * zw_ref[:, hs] * o_w[rs])
        out_ref[:, hs] = merged.astype(out_ref.dtype)


def _nsa_sw(flags, q, sel, kv_att, zg, gates, oc, batch, seq):
    t = q.shape[0]
    tq = Q_TILE
    nqb = seq // tq
    nch = seq // SEL_CHUNK
    nsp = sel.shape[3]
    gw = B_HPG * B_DK
    rows = B_HPG * tq
    tok = lambda j: pl.BlockSpec((tq, gw), lambda b, g, i, f, j=j: (b * nqb + i, j * B_GROUPS + g))
    res = lambda k: pl.BlockSpec((1, 1, seq, B_DK), lambda b, g, i, f, k=k: (k * B_GROUPS + g, b, 0, 0))
    grid_spec = pltpu.PrefetchScalarGridSpec(
        num_scalar_prefetch=1,
        grid=(batch, B_GROUPS, nqb),
        in_specs=[tok(0),
                  pl.BlockSpec((1, 1, tq, nsp), lambda b, g, i, f: (b, g, i, 0)),
                  res(0), res(1), res(2), res(3),
                  tok(1), tok(2),
                  pl.BlockSpec((tq, LANES), lambda b, g, i, f: (b * nqb + i, g)),
                  tok(0)],
        out_specs=tok(0),
        scratch_shapes=[pltpu.VMEM((rows, B_DK + nsp), BF16),
                        pltpu.VMEM((rows, SEL_CHUNK), F32),
                        pltpu.VMEM((rows, SEL_CHUNK), F32),
                        pltpu.VMEM((rows, WINDOW + max(tq, SEL_CHUNK)), F32),
                        pltpu.VMEM((rows, LANES), F32),
                        pltpu.VMEM((rows, LANES), F32),
                        pltpu.VMEM((rows, B_DK), F32)])
    return pl.pallas_call(
        functools.partial(_nsa_sw_body, tq=tq, nqb=nqb, nch=nch),
        grid_spec=grid_spec,
        out_shape=jax.ShapeDtypeStruct((t, B_GROUPS * gw), BF16),
        compiler_params=_params(3),
        name="nsa_sw",
    )(flags, q, sel, kv_att, kv_att, kv_att, kv_att, zg, zg, gates, oc)


def _blockdiag_dense(w):
    nb, bs, _ = w.shape
    per = MXU_DIM // bs
    w4 = w.reshape(nb // per, per, bs, bs)
    dense = jnp.einsum("sgij,gh->sgihj", w4, jnp.eye(per, dtype=w.dtype))
    return dense.reshape(nb // per, MXU_DIM, MXU_DIM).astype(BF16)


def _pad_cols(w, width):
    return jnp.pad(w, ((0, 0), (0, width - w.shape[1])))


def _overlap_matrix_t(nr, nsp, nc, ns):
    i = jnp.arange(nr)[None, :] * CMP_STRIDE
    j = jnp.arange(nsp)[:, None] * SEL_BLK
    ov = jnp.minimum(i + CMP_BLK, j + SEL_BLK) - jnp.maximum(i, j)
    ov = jnp.maximum(ov, 0) / CMP_STRIDE
    live = (jnp.arange(nr)[None, :] < nc) & (jnp.arange(nsp)[:, None] < ns)
    return jnp.where(live, ov, 0).astype(BF16)


def _mlstm_layer(xf, batch, seq, g_pre, g_post, w_in, conv_w, conv_b, w_q, w_k, w_v,
                 w_gate, b_gate, head_norm, skip, w_out):
    c = conv_w.shape[1]
    dh = c // A_HEADS
    proj = _rms_matmul(xf, g_pre, w_in.astype(BF16), F32)
    wg = [_pad_cols(w_gate[j * c:(j + 1) * c], LANES).astype(BF16) for j in range(3)]
    bg = _pad_cols(b_gate.reshape(1, -1), LANES).astype(F32)
    q, k, v, xc, gates = _a_local(
        proj, seq, conv_w.astype(F32), conv_b.reshape(1, c).astype(F32),
        _blockdiag_dense(w_q), _blockdiag_dense(w_k), _blockdiag_dense(w_v),
        wg[0], wg[1], wg[2], bg, float(dh) ** -0.5)
    y = _a_cell(q, k, v, xc, proj, gates, head_norm.reshape(1, c).astype(F32),
                skip.reshape(1, c).astype(F32), batch, seq)
    return _out_proj(y, w_out.astype(BF16), g_post, xf)


def _shared_kv(xf, batch, seq, kv_norm, kv_w, cmp_pos, cmp_w1, cmp_b1, cmp_w2, cmp_b2):
    ncmp = 2 * B_GROUPS * B_DK
    kv_w = kv_w.astype(BF16)
    kv_cmp, h = _rms_matmul(xf, kv_norm, kv_w[:, :ncmp], F32, split=True, emit_h=True)
    kv_att = _matmul(h, kv_w[:, ncmp:], BF16, split=True)
    nr = seq // CMP_STRIDE
    half = CMP_STRIDE * B_DK
    pos_a = cmp_pos[:, :CMP_STRIDE].reshape(2, 1, half).astype(F32)
    pos_b = cmp_pos[:, CMP_STRIDE:].reshape(2, 1, half).astype(F32)
    kvc = _kv_compress(kv_cmp.reshape(2 * B_GROUPS, batch, nr, half), pos_a, pos_b,
                       cmp_w1[:, :half].astype(BF16), cmp_w1[:, half:].astype(BF16),
                       cmp_b1.reshape(2, 1, -1).astype(F32), cmp_w2.astype(BF16),
                       cmp_b2.reshape(2, 1, -1).astype(F32))
    return kvc, kv_att.reshape(4 * B_GROUPS, batch, seq, B_DK)


def _nsa_layer(xf, batch, seq, g_pre, g_post, w_in, w_out, shared, wov_t, n_sel):
    kvc, kv_att = shared
    nqk = B_HEADS * B_DK
    ngl = N_BRANCH * B_HEADS
    w_q = w_in[:, :nqk].astype(BF16)
    w_gl = w_in[:, nqk:nqk + ngl].reshape(-1, N_BRANCH, B_GROUPS, B_HPG)
    w_gl = jnp.transpose(w_gl, (0, 2, 1, 3)).reshape(-1, B_GROUPS, N_BRANCH * B_HPG)
    w_gl = jnp.pad(w_gl, ((0, 0), (0, 0), (0, LANES - N_BRANCH * B_HPG))).reshape(-1, B_GROUPS * LANES)
    q, h = _rms_matmul(xf, g_pre, w_q, BF16, scale=LOG2E * float(B_DK) ** -0.5, emit_h=True)
    zg = _matmul(h, w_in[:, nqk + ngl:].astype(BF16), F32, act="silu")
    gates = _matmul(h, w_gl.astype(BF16), F32, act="sigmoid")
    oc, sel, blk_any = _nsa_cmp(q, kvc, zg, gates, wov_t, batch, seq, n_sel)
    ns = seq // SEL_BLK
    bpc = SEL_CHUNK // SEL_BLK
    flags = blk_any[:, :, :, 0, :ns].reshape(batch, B_GROUPS, seq // Q_TILE, ns // bpc, bpc)
    flags = (jnp.max(flags, axis=-1) > 0).astype(jnp.int32)
    order = jnp.argsort(1 - flags, axis=-1, stable=True).astype(jnp.int32)
    count = jnp.sum(flags, axis=-1, keepdims=True)
    lists = jnp.concatenate([count, order, jnp.zeros_like(count)], axis=-1).reshape(-1)
    out = _nsa_sw(lists, q, sel, kv_att, zg, gates, oc, batch, seq)
    return _out_proj(out, w_out.astype(BF16), g_post, xf)


def kernel(x, norm_pre, norm_post, a_w_in, a_conv_w, a_conv_b, a_w_q, a_w_k, a_w_v, a_w_gate,
           a_b_gate, a_head_norm, a_skip, a_w_out, kv_norm, kv_w, cmp_pos, cmp_w1, cmp_b1,
           cmp_w2, cmp_b2, b_w_in, b_w_out):
    batch, seq, d = x.shape
    na = a_w_in.shape[0]
    nb = b_w_in.shape[0]
    assert seq % (CMP_STRIDE * LANES) == 0 and seq % SEL_CHUNK == 0 and seq % A_CHUNK == 0
    xf = x.reshape(batch * seq, d).astype(F32)
    for l in range(na):
        xf = _mlstm_layer(xf, batch, seq, norm_pre[l], norm_post[l], a_w_in[l], a_conv_w[l],
                          a_conv_b[l], a_w_q[l], a_w_k[l], a_w_v[l], a_w_gate[l], a_b_gate[l],
                          a_head_norm[l], a_skip[l], a_w_out[l])
    shared = _shared_kv(xf, batch, seq, kv_norm, kv_w, cmp_pos, cmp_w1, cmp_b1, cmp_w2, cmp_b2)
    nr = seq // CMP_STRIDE
    ns = seq // SEL_BLK
    nsp = -(-ns // LANES) * LANES
    wov_t = _overlap_matrix_t(nr, nsp, nr - CMP_BLK // CMP_STRIDE + 1, ns)
    for l in range(nb):
        xf = _nsa_layer(xf, batch, seq, norm_pre[na + l], norm_post[na + l], b_w_in[l],
                        b_w_out[l], shared, wov_t, min(SEL_TOP, ns))
    return xf.reshape(batch, seq, d).astype(x.dtype)
```
